```python
import math
import jax, jax.numpy as jnp
from jax import lax
import numpy as np

D_MODEL = 1024
BATCH = 4
SEQ = 8192
DEPTH = 2

GRID_W = 64
CTX_LEN = 256
N_EVEN = (DEPTH + 1) // 2
N_ODD = DEPTH // 2

S5_WIDTH = D_MODEL // 2
S5_GROUP = 16
S5_GROUPS = S5_WIDTH // S5_GROUP
S5_STATE = 64
RG_WIDTH = D_MODEL // 2
RG_HEADS = 8
RG_HEAD_DIM = RG_WIDTH // RG_HEADS
RG_C = 8.0
CONV_K = 4
AB_IN = S5_WIDTH + 2 * RG_WIDTH
AB_OUT = S5_WIDTH + RG_WIDTH

M2_INNER = 2 * D_MODEL
M2_HEAD_DIM = 64
M2_HEADS = M2_INNER // M2_HEAD_DIM
M2_GROUPS = 8
M2_HPG = M2_HEADS // M2_GROUPS
M2_STATE = 128
M2_CHUNK = 128
M2_CONV_DIM = M2_INNER + 2 * M2_GROUPS * M2_STATE
M2_IN = M2_INNER + M2_CONV_DIM + 2 * M2_HEADS

N_EXPERTS = 16
N_EXPERT_GROUPS = 4
EXPERTS_PER_GROUP = N_EXPERTS // N_EXPERT_GROUPS
TOP_K = 2
D_EXPERT = D_MODEL
MOE_BLOCK = 256

EPS = 1e-6

kernel_name = 'hybrid_s5_rglru_ssd_moe_prefix_dit'


def rmsnorm(x, w):
    xf = x.astype(jnp.float32)
    y = xf * lax.rsqrt(jnp.mean(xf * xf, axis=-1, keepdims=True) + EPS)
    return (y * w.astype(jnp.float32)).astype(x.dtype)


def dwconv_centred(x, w, b):
    k = w.shape[0]
    y = lax.conv_general_dilated(x, w[:, None, :].astype(x.dtype), window_strides=(1,),
                                 padding=[((k - 1) // 2, k // 2)],
                                 dimension_numbers=('NWC', 'WIO', 'NWC'),
                                 feature_group_count=x.shape[-1])
    return y + b.astype(y.dtype)


def s5_discretize(lam_re, lam_im, log_dt, b_re, b_im):
    lam_re = lam_re.astype(jnp.float32); lam_im = lam_im.astype(jnp.float32)
    b_re = b_re.astype(jnp.float32); b_im = b_im.astype(jnp.float32)
    dt = jnp.exp(log_dt.astype(jnp.float32))[:, None]
    mag = jnp.exp(lam_re * dt)
    ar = mag * jnp.cos(lam_im * dt)
    ai = mag * jnp.sin(lam_im * dt)
    den = lam_re * lam_re + lam_im * lam_im
    fr = ((ar - 1.0) * lam_re + ai * lam_im) / den
    fi = (ai * lam_re - (ar - 1.0) * lam_im) / den
    bbr = fr[..., None] * b_re - fi[..., None] * b_im
    bbi = fr[..., None] * b_im + fi[..., None] * b_re
    return ar, ai, bbr, bbi


def _complex_combine(left, right):
    ar1, ai1, br1, bi1 = left
    ar2, ai2, br2, bi2 = right
    return (ar2 * ar1 - ai2 * ai1,
            ar2 * ai1 + ai2 * ar1,
            ar2 * br1 - ai2 * bi1 + br2,
            ar2 * bi1 + ai2 * br1 + bi2)


def s5_scan(u, h0, lam_re, lam_im, log_dt, b_re, b_im, reverse):
    ar, ai, bbr, bbi = s5_discretize(lam_re, lam_im, log_dt, b_re, b_im)
    n = u.shape[1]
    bur = jnp.einsum('bngk,gpk->nbgp', u, bbr)
    bui = jnp.einsum('bngk,gpk->nbgp', u, bbi)
    if h0 is not None:
        h0r, h0i = h0
        idx = n - 1 if reverse else 0
        bur = bur.at[idx].add(ar * h0r - ai * h0i)
        bui = bui.at[idx].add(ar * h0i + ai * h0r)
    a_r = jnp.broadcast_to(ar, (n, 1) + ar.shape)
    a_i = jnp.broadcast_to(ai, (n, 1) + ai.shape)
    _, _, hr, hi = lax.associative_scan(_complex_combine, (a_r, a_i, bur, bui), reverse=reverse, axis=0)
    last = 0 if reverse else n - 1
    return (hr, hi), (hr[last], hi[last])


def s5_readout(h, c_re, c_im):
    hr, hi = h
    return (jnp.einsum('nbgp,gkp->bngk', hr, c_re.astype(jnp.float32))
            - jnp.einsum('nbgp,gkp->bngk', hi, c_im.astype(jnp.float32)))


def _real_combine(left, right):
    a1, b1 = left
    a2, b2 = right
    return a2 * a1, a2 * b1 + b2


def rglru_scan(xc, h0, wa, ba, wx, bx, lam, reverse):
    b_, n, _ = xc.shape
    xh = xc.reshape(b_, n, RG_HEADS, RG_HEAD_DIM)
    r = jax.nn.sigmoid(jnp.einsum('bnhi,hij->bnhj', xh, wa.astype(jnp.float32)).reshape(b_, n, RG_WIDTH) + ba)
    i = jax.nn.sigmoid(jnp.einsum('bnhi,hij->bnhj', xh, wx.astype(jnp.float32)).reshape(b_, n, RG_WIDTH) + bx)
    log_a = -RG_C * r * jax.nn.softplus(-lam.astype(jnp.float32))
    a = jnp.exp(log_a)
    beta = jnp.sqrt(-jnp.expm1(2.0 * log_a))
    bterm = beta * i * xc
    if h0 is not None:
        idx = n - 1 if reverse else 0
        bterm = bterm.at[:, idx].add(a[:, idx] * h0)
    _, h = lax.associative_scan(_real_combine, (a, bterm), reverse=reverse, axis=1)
    last = 0 if reverse else n - 1
    return h, h[:, last]


def mixer_s5_rglru(h_c, h_l, need_ctx, w_in, w_out, lam_re, lam_im, log_dt, b_re, b_im, c_re, c_im,
                   s5_d, w_glu, conv_w, conv_b, wa, ba, wx, bx, lam):
    def project(h):
        u, xr, gr = jnp.split(h @ w_in, [S5_WIDTH, S5_WIDTH + RG_WIDTH], axis=-1)
        ug = u.astype(jnp.float32).reshape(h.shape[0], h.shape[1], S5_GROUPS, S5_GROUP)
        xc = dwconv_centred(xr, conv_w, conv_b).astype(jnp.float32)
        return ug, xc, gr
    ug_c, xc_c, gr_c = project(h_c)
    ug_l, xc_l, gr_l = project(h_l)
    s5_dir = [(lam_re[d], lam_im[d], log_dt[d], b_re[d], b_im[d]) for d in range(2)]
    rg_dir = [(wa[d], ba[d], wx[d], bx[d], lam[d]) for d in range(2)]
    sc_f, s5_hf = s5_scan(ug_c, None, *s5_dir[0], reverse=False)
    sc_b, s5_hb = s5_scan(ug_c, None, *s5_dir[1], reverse=True)
    sl_f, _ = s5_scan(ug_l, s5_hf, *s5_dir[0], reverse=False)
    sl_b, _ = s5_scan(ug_l, s5_hb, *s5_dir[1], reverse=True)
    rc_f, rg_hf = rglru_scan(xc_c, None, *rg_dir[0], reverse=False)
    rc_b, rg_hb = rglru_scan(xc_c, None, *rg_dir[1], reverse=True)
    rl_f, _ = rglru_scan(xc_l, rg_hf, *rg_dir[0], reverse=False)
    rl_b, _ = rglru_scan(xc_l, rg_hb, *rg_dir[1], reverse=True)

    def merge(ug, sf, sb, rf, rb, gr, dtype):
        b_, n = ug.shape[:2]
        y = (s5_readout(sf, c_re[0], c_im[0]) + s5_readout(sb, c_re[1], c_im[1])
             + s5_d.astype(jnp.float32).reshape(S5_GROUPS, S5_GROUP) * ug)
        y = jax.nn.gelu(y.reshape(b_, n, S5_WIDTH))
        y = y * jax.nn.sigmoid(y @ w_glu.astype(jnp.float32))
        g = (rf + rb) * jax.nn.gelu(gr.astype(jnp.float32))
        return jnp.concatenate([y, g], axis=-1).astype(dtype) @ w_out
    out_l = merge(ug_l, sl_f, sl_b, rl_f, rl_b, gr_l, h_l.dtype)
    out_c = merge(ug_c, sc_f, sc_b, rc_f, rc_b, gr_c, h_c.dtype) if need_ctx else None
    return out_c, out_l


def ssd_scan(x, dt, A, B, C, h0, need_y):
    b_, l = x.shape[:2]
    q = M2_CHUNK
    c = l // q
    xdt = (x * dt[..., None]).reshape(b_, c, q, M2_GROUPS, M2_HPG, M2_HEAD_DIM)
    a_cs = jnp.cumsum((dt * A).reshape(b_, c, q, M2_GROUPS, M2_HPG), axis=2)
    Bc = B.reshape(b_, c, q, M2_GROUPS, M2_STATE)
    Cc = C.reshape(b_, c, q, M2_GROUPS, M2_STATE)
    a_tot = a_cs[:, :, -1]
    decay_to_end = jnp.exp(a_tot[:, :, None] - a_cs)
    states = jnp.einsum('bcsgn,bcsgr,bcsgrp->bcgrpn', Bc, decay_to_end, xdt)
    if h0 is None:
        h0 = jnp.zeros((b_, M2_GROUPS, M2_HPG, M2_HEAD_DIM, M2_STATE), jnp.float32)

    def chunk_step(s, inp):
        st, at = inp
        return jnp.exp(at)[..., None, None] * s + st, s
    final, states_in = lax.scan(chunk_step, h0, (jnp.moveaxis(states, 1, 0), jnp.moveaxis(a_tot, 1, 0)))
    if not need_y:
        return None, final
    seg = a_cs[:, :, :, None] - a_cs[:, :, None, :]
    mask = jnp.tril(jnp.ones((q, q), dtype=bool))[:, :, None, None]
    lmat = jnp.exp(jnp.where(mask, seg, -jnp.inf))
    cb = jnp.einsum('bcqgn,bcsgn->bcqsg', Cc, Bc)
    y_diag = jnp.einsum('bcqsg,bcqsgr,bcsgrp->bcqgrp', cb, lmat, xdt)
    y_off = jnp.einsum('bcqgn,cbgrpn,bcqgr->bcqgrp', Cc, states_in, jnp.exp(a_cs))
    y = (y_diag + y_off).reshape(b_, l, M2_HEADS, M2_HEAD_DIM)
    return y, final


def mixer_ssd(h_c, h_l, need_ctx, w_in, conv_w, conv_b, dt_bias, a_log, d_skip, norm_w, w_out):
    dtb = dt_bias.astype(jnp.float32)

    def project(h):
        b_, n = h.shape[:2]
        z, xbc, dt = jnp.split(h @ w_in, [M2_INNER, M2_INNER + M2_CONV_DIM], axis=-1)
        xbc = jax.nn.silu(dwconv_centred(xbc, conv_w, conv_b)).astype(jnp.float32)
        xs, Bs, Cs = jnp.split(xbc, [M2_INNER, M2_INNER + M2_GROUPS * M2_STATE], axis=-1)
        xs = xs.reshape(b_, n, M2_HEADS, M2_HEAD_DIM)
        Bs = Bs.reshape(b_, n, M2_GROUPS, M2_STATE)
        Cs = Cs.reshape(b_, n, M2_GROUPS, M2_STATE)
        dt = dt.astype(jnp.float32)
        dt_f = jax.nn.softplus(dt[..., :M2_HEADS] + dtb[0])
        dt_b = jax.nn.softplus(dt[..., M2_HEADS:] + dtb[1])
        return z, xs, Bs, Cs, dt_f, dt_b
    A = -jnp.exp(a_log.astype(jnp.float32))
    flip = lambda t: jnp.flip(t, axis=1)
    zc, xc, Bc, Cc, dfc, dbc = project(h_c)
    zl, xl, Bl, Cl, dfl, dbl = project(h_l)
    yc_f, st_f = ssd_scan(xc, dfc, A[0], Bc, Cc, None, need_ctx)
    yc_b, st_b = ssd_scan(flip(xc), flip(dbc), A[1], flip(Bc), flip(Cc), None, need_ctx)
    yl_f, _ = ssd_scan(xl, dfl, A[0], Bl, Cl, st_f, True)
    yl_b, _ = ssd_scan(flip(xl), flip(dbl), A[1], flip(Bl), flip(Cl), st_b, True)

    def finish(yf, yb_rev, xs, z, dtype):
        b_, n = xs.shape[:2]
        y = yf + flip(yb_rev) + d_skip.astype(jnp.float32)[:, None] * xs
        y = y.reshape(b_, n, M2_INNER) * jax.nn.silu(z.astype(jnp.float32))
        return rmsnorm(y, norm_w).astype(dtype) @ w_out
    out_l = finish(yl_f, yl_b, xl, zl, h_l.dtype)
    out_c = finish(yc_f, yc_b, xc, zc, h_c.dtype) if need_ctx else None
    return out_c, out_l


def route(h, router_w, router_b):
    logits = (h @ router_w).astype(jnp.float32)
    probs = jax.nn.softmax(logits, axis=-1)
    sel = probs + router_b.astype(jnp.float32)
    grp_score = sel.reshape(-1, N_EXPERT_GROUPS, EXPERTS_PER_GROUP).max(axis=-1)
    best = jnp.argmax(grp_score, axis=-1)
    in_grp = (jnp.arange(N_EXPERTS) // EXPERTS_PER_GROUP)[None, :] == best[:, None]
    _, idx = lax.top_k(jnp.where(in_grp, sel, -jnp.inf), TOP_K)
    gate = jnp.take_along_axis(probs, idx, axis=-1)
    gate = gate / jnp.sum(gate, axis=-1, keepdims=True)
    return idx, gate


def moe(h, router_w, router_b, w1, w3, w2):
    T, D = h.shape
    idx, gate = route(h, router_w, router_b)
    M = T * TOP_K
    e_flat = idx.reshape(M)
    order = jnp.argsort(e_flat)
    e_s = e_flat[order]
    tok_s = order // TOP_K
    g_s = gate.reshape(M)[order]
    counts = jnp.bincount(e_flat, length=N_EXPERTS)
    padded = (counts + MOE_BLOCK - 1) // MOE_BLOCK * MOE_BLOCK
    start = jnp.cumsum(counts) - counts
    pend = jnp.cumsum(padded)
    pstart = pend - padded
    dest = pstart[e_s] + jnp.arange(M) - start[e_s]
    n_blocks = (M + N_EXPERTS * (MOE_BLOCK - 1)) // MOE_BLOCK + 1
    buf = jnp.zeros((n_blocks * MOE_BLOCK, D), h.dtype).at[dest].set(h[tok_s])
    block_expert = jnp.minimum(jnp.searchsorted(pend, jnp.arange(n_blocks) * MOE_BLOCK, side='right'),
                               N_EXPERTS - 1)

    def expert_block(args):
        xb, e = args
        return (jax.nn.silu(xb @ w1[e]) * (xb @ w3[e])) @ w2[e]
    yb = lax.map(expert_block, (buf.reshape(n_blocks, MOE_BLOCK, D), block_expert))
    y = yb.reshape(n_blocks * MOE_BLOCK, -1)[dest] * g_s[:, None].astype(yb.dtype)
    return jax.ops.segment_sum(y, tok_s, num_segments=T)


def setup_inputs(seed: int = 0) -> dict:
    key = jax.random.key(seed)
    ks = iter(jax.random.split(key, 48))
    f32 = jnp.float32
    nrm = lambda shape, s: jax.random.normal(next(ks), shape, f32) * s
    uni = lambda shape, lo, hi: jax.random.uniform(next(ks), shape, f32, lo, hi)
    D = D_MODEL
    x = nrm((BATCH, SEQ, D), 1.0)
    c = nrm((BATCH, D), 1.0)
    ctx = nrm((BATCH, CTX_LEN, D), 1.0)
    c_ctx = nrm((D,), 1.0)
    w_mod = nrm((DEPTH, D, 6 * D), 0.5 * D ** -0.5)
    b_mod = nrm((DEPTH, 6 * D), 0.01)
    norm_mix = 1.0 + nrm((DEPTH, D), 0.01)
    norm_ffn = 1.0 + nrm((DEPTH, D), 0.01)
    norm_final = 1.0 + nrm((D,), 0.01)
    router_w = nrm((D, N_EXPERTS), D ** -0.5)
    router_b = nrm((N_EXPERTS,), 0.01)
    exp_w1 = nrm((DEPTH, N_EXPERTS, D, D_EXPERT), D ** -0.5)
    exp_w3 = nrm((DEPTH, N_EXPERTS, D, D_EXPERT), D ** -0.5)
    exp_w2 = nrm((DEPTH, N_EXPERTS, D_EXPERT, D), D_EXPERT ** -0.5)
    ab_w_in = nrm((N_EVEN, D, AB_IN), D ** -0.5)
    ab_w_out = nrm((N_EVEN, AB_OUT, D), AB_OUT ** -0.5)
    s5_shape = (N_EVEN, 2, S5_GROUPS, S5_STATE)
    s5_lam_re = -0.5 + nrm(s5_shape, 0.01)
    s5_lam_im = jnp.pi * jnp.arange(S5_STATE, dtype=f32) + nrm(s5_shape, 0.01)
    s5_log_dt = uni((N_EVEN, 2, S5_GROUPS), math.log(1e-3), math.log(1e-1))
    s5_b_re = nrm(s5_shape + (S5_GROUP,), (2 * S5_GROUP) ** -0.5)
    s5_b_im = nrm(s5_shape + (S5_GROUP,), (2 * S5_GROUP) ** -0.5)
    s5_c_re = nrm((N_EVEN, 2, S5_GROUPS, S5_GROUP, S5_STATE), (2 * S5_STATE) ** -0.5)
    s5_c_im = nrm((N_EVEN, 2, S5_GROUPS, S5_GROUP, S5_STATE), (2 * S5_STATE) ** -0.5)
    s5_d = nrm((N_EVEN, S5_WIDTH), 1.0)
    s5_w_glu = nrm((N_EVEN, S5_WIDTH, S5_WIDTH), S5_WIDTH ** -0.5)
    rg_conv_w = nrm((N_EVEN, CONV_K, RG_WIDTH), CONV_K ** -0.5)
    rg_conv_b = nrm((N_EVEN, RG_WIDTH), 0.01)
    rg_wa = nrm((N_EVEN, 2, RG_HEADS, RG_HEAD_DIM, RG_HEAD_DIM), RG_HEAD_DIM ** -0.5)
    rg_ba = nrm((N_EVEN, 2, RG_WIDTH), 0.01)
    rg_wx = nrm((N_EVEN, 2, RG_HEADS, RG_HEAD_DIM, RG_HEAD_DIM), RG_HEAD_DIM ** -0.5)
    rg_bx = nrm((N_EVEN, 2, RG_WIDTH), 0.01)
    a_c = uni((N_EVEN, 2, RG_WIDTH), 0.9, 0.999) ** (1.0 / RG_C)
    rg_lam = jnp.log(a_c) - jnp.log1p(-a_c)
    m2_w_in = nrm((N_ODD, D, M2_IN), D ** -0.5)
    m2_conv_w = nrm((N_ODD, CONV_K, M2_CONV_DIM), CONV_K ** -0.5)
    m2_conv_b = nrm((N_ODD, M2_CONV_DIM), 0.01)
    dt0 = jnp.exp(uni((N_ODD, 2, M2_HEADS), math.log(1e-3), math.log(1e-1)))
    m2_dt_bias = dt0 + jnp.log(-jnp.expm1(-dt0))
    m2_a_log = jnp.log(uni((N_ODD, 2, M2_HEADS), 1.0, 16.0))
    m2_d = 1.0 + nrm((N_ODD, M2_HEADS), 0.01)
    m2_norm = 1.0 + nrm((N_ODD, M2_INNER), 0.01)
    m2_w_out = nrm((N_ODD, M2_INNER, D), M2_INNER ** -0.5)
    return {'x': x, 'c': c, 'ctx': ctx, 'c_ctx': c_ctx, 'w_mod': w_mod, 'b_mod': b_mod,
            'norm_mix': norm_mix, 'norm_ffn': norm_ffn, 'norm_final': norm_final,
            'router_w': router_w, 'router_b': router_b,
            'exp_w1': exp_w1, 'exp_w3': exp_w3, 'exp_w2': exp_w2,
            'ab_w_in': ab_w_in, 'ab_w_out': ab_w_out,
            's5_lam_re': s5_lam_re, 's5_lam_im': s5_lam_im, 's5_log_dt': s5_log_dt,
            's5_b_re': s5_b_re, 's5_b_im': s5_b_im, 's5_c_re': s5_c_re, 's5_c_im': s5_c_im,
            's5_d': s5_d, 's5_w_glu': s5_w_glu,
            'rg_conv_w': rg_conv_w, 'rg_conv_b': rg_conv_b, 'rg_wa': rg_wa, 'rg_ba': rg_ba,
            'rg_wx': rg_wx, 'rg_bx': rg_bx, 'rg_lam': rg_lam,
            'm2_w_in': m2_w_in, 'm2_conv_w': m2_conv_w, 'm2_conv_b': m2_conv_b,
            'm2_dt_bias': m2_dt_bias, 'm2_a_log': m2_a_log, 'm2_d': m2_d, 'm2_norm': m2_norm,
            'm2_w_out': m2_w_out}


def reference(x, c, ctx, c_ctx, w_mod, b_mod, norm_mix, norm_ffn, norm_final, router_w, router_b,
              exp_w1, exp_w3, exp_w2, ab_w_in, ab_w_out, s5_lam_re, s5_lam_im, s5_log_dt,
              s5_b_re, s5_b_im, s5_c_re, s5_c_im, s5_d, s5_w_glu, rg_conv_w, rg_conv_b, rg_wa, rg_ba,
              rg_wx, rg_bx, rg_lam, m2_w_in, m2_conv_w, m2_conv_b, m2_dt_bias, m2_a_log, m2_d,
              m2_norm, m2_w_out):
    b_, L, D = x.shape
    n_ctx = ctx.shape[1]
    rows = L // GRID_W

    def to_col(t):
        return t.reshape(b_, rows, GRID_W, D).transpose(0, 2, 1, 3).reshape(b_, L, D)

    def to_row(t):
        return t.reshape(b_, GRID_W, rows, D).transpose(0, 2, 1, 3).reshape(b_, L, D)

    x_l, x_c = x, ctx
    for layer in range(DEPTH):
        need_ctx = layer < DEPTH - 1
        mod_l = (jax.nn.silu(c) @ w_mod[layer] + b_mod[layer])[:, None, :]
        mod_c = (jax.nn.silu(c_ctx) @ w_mod[layer] + b_mod[layer])[None, None, :]
        sh1_l, sc1_l, g1_l, sh2_l, sc2_l, g2_l = jnp.split(mod_l, 6, axis=-1)
        sh1_c, sc1_c, g1_c, sh2_c, sc2_c, g2_c = jnp.split(mod_c, 6, axis=-1)
        h_l = rmsnorm(x_l, norm_mix[layer]) * (1.0 + sc1_l) + sh1_l
        h_c = rmsnorm(x_c, norm_mix[layer]) * (1.0 + sc1_c) + sh1_c
        j = layer // 2
        if layer % 2 == 0:
            m_c, m_l = mixer_s5_rglru(h_c, h_l, need_ctx, ab_w_in[j], ab_w_out[j],
                                      s5_lam_re[j], s5_lam_im[j], s5_log_dt[j], s5_b_re[j], s5_b_im[j],
                                      s5_c_re[j], s5_c_im[j], s5_d[j], s5_w_glu[j],
                                      rg_conv_w[j], rg_conv_b[j], rg_wa[j], rg_ba[j], rg_wx[j], rg_bx[j],
                                      rg_lam[j])
        else:
            m_c, m_col = mixer_ssd(h_c, to_col(h_l), need_ctx, m2_w_in[j], m2_conv_w[j], m2_conv_b[j],
                                   m2_dt_bias[j], m2_a_log[j], m2_d[j], m2_norm[j], m2_w_out[j])
            m_l = to_row(m_col)
        x_l = x_l + g1_l * m_l
        h_l = rmsnorm(x_l, norm_ffn[layer]) * (1.0 + sc2_l) + sh2_l
        if need_ctx:
            x_c = x_c + g1_c * m_c
            h_c = rmsnorm(x_c, norm_ffn[layer]) * (1.0 + sc2_c) + sh2_c
            tokens = jnp.concatenate([h_c, h_l], axis=1)
            f = moe(tokens.reshape(-1, D), router_w, router_b,
                    exp_w1[layer], exp_w3[layer], exp_w2[layer]).reshape(b_, n_ctx + L, D)
            x_c = x_c + g2_c * f[:, :n_ctx]
            x_l = x_l + g2_l * f[:, n_ctx:]
        else:
            f = moe(h_l.reshape(-1, D), router_w, router_b,
                    exp_w1[layer], exp_w3[layer], exp_w2[layer]).reshape(b_, L, D)
            x_l = x_l + g2_l * f
    return rmsnorm(x_l, norm_final)
```

```python
import functools
import math

import jax
import jax.numpy as jnp
from jax import lax
from jax.experimental import pallas as pl
from jax.experimental.pallas import tpu as pltpu

F32 = jnp.float32
BF16 = jnp.bfloat16
I32 = jnp.int32

D_MODEL = 1024
GRID_W = 64
EPS = 1e-6

S5_WIDTH = 512
S5_GROUP = 16
S5_GROUPS = 32
S5_STATE = 64
RG_WIDTH = 512
RG_HEADS = 8
RG_HEAD_DIM = 64
RG_C = 8.0
CONV_K = 4
HALF_GROUPS = S5_GROUPS // 2
HALF_STATE = HALF_GROUPS * S5_STATE
HALF_RG = RG_WIDTH // 2

M2_INNER = 2048
M2_HEAD_DIM = 64
M2_HEADS = 32
M2_GROUPS = 8
M2_HPG = 4
M2_STATE = 128
M2_CHUNK = 128
M2_CONV_DIM = M2_INNER + 2 * M2_GROUPS * M2_STATE

N_EXPERTS = 16
N_EXPERT_GROUPS = 4
EXPERTS_PER_GROUP = 4
N_CLASSES = 24
MOE_BLOCK = 256
ROW_W = D_MODEL + 128

TOK_TILE = 256
HALO = 8
SCAN_TILE = 128
VMEM_LIMIT = 52 * 1024 * 1024


def _cp(sem, vmem=VMEM_LIMIT):
    return pltpu.CompilerParams(dimension_semantics=sem, vmem_limit_bytes=vmem)


def _dot(a, b):
    return jnp.dot(a, b, preferred_element_type=F32)


def _split2(a):
    hi = a.astype(BF16)
    lo = (a - hi.astype(F32)).astype(BF16)
    return hi, lo


def _split3(a):
    a1 = a.astype(BF16)
    r1 = a - a1.astype(F32)
    a2 = r1.astype(BF16)
    a3 = (r1 - a2.astype(F32)).astype(BF16)
    return a1, a2, a3


def _dot3(a, b):
    ah, al = _split2(a)
    bh, bl = _split2(b)
    return _dot(ah, bh) + (_dot(al, bh) + _dot(ah, bl))


def _norm_mod(x, scale, shift):
    ms = jnp.mean(x * x, axis=-1, keepdims=True)
    return (x * lax.rsqrt(ms + EPS)) * scale + shift


def _silu(x):
    return x * jax.nn.sigmoid(x)


def _gelu(x):
    c = math.sqrt(2.0 / math.pi)
    return 0.5 * x * (1.0 + jnp.tanh(c * (x + 0.044715 * (x * x * x))))


def _softplus(x):
    return jnp.maximum(x, 0.0) + jnp.log(1.0 + jnp.exp(-jnp.abs(x)))


def _mod_kernel(c_ref, w_ref, b_ref, o_ref):
    c = c_ref[...]
    o_ref[...] = _dot3(_silu(c), w_ref[...]) + b_ref[...]


def _modulation(c8, w_mod, b_mod):
    depth, d, n = w_mod.shape
    tn = 1536
    return pl.pallas_call(
        _mod_kernel,
        grid=(depth, n // tn),
        in_specs=[pl.BlockSpec((8, d), lambda l, j: (0, 0)),
                  pl.BlockSpec((None, d, tn), lambda l, j: (l, 0, j)),
                  pl.BlockSpec((None, 1, tn), lambda l, j: (l, 0, j))],
        out_specs=pl.BlockSpec((None, 8, tn), lambda l, j: (l, 0, j)),
        out_shape=jax.ShapeDtypeStruct((depth, 8, n), F32),
        compiler_params=_cp(("parallel", "parallel")),
        name="modulation",
    )(c8, w_mod, b_mod.reshape(depth, 1, n))


def _inproj0_kernel(xp_ref, xc_ref, xn_ref, sc_ref, sh_ref, w_ref, cw_ref, cb_ref,
                    u_ref, ua_ref, xcf_ref, xca_ref, gr_ref, xr_s, *, nlt):
    i = pl.program_id(1)
    tt = TOK_TILE
    has_prev = jnp.logical_and(i > 0, i < nlt).astype(F32)
    has_next = (i < nlt - 1).astype(F32)
    xe = jnp.concatenate([xp_ref[...], xc_ref[...], xn_ref[...]], axis=0)
    h = _norm_mod(xe, sc_ref[...], sh_ref[...]).astype(BF16)
    p = _dot(h, w_ref[...])
    u = p[HALO:HALO + tt, 0:S5_WIDTH]
    gr = p[HALO:HALO + tt, S5_WIDTH + RG_WIDTH:]
    row = lax.broadcasted_iota(I32, (tt + 2 * HALO, 1), 0)
    keep = jnp.where(row < HALO, has_prev, jnp.where(row >= tt + HALO, has_next, 1.0))
    xr_s[...] = p[:, S5_WIDTH:S5_WIDTH + RG_WIDTH] * keep
    xc = cb_ref[...] + cw_ref[0:1, :] * xr_s[pl.ds(HALO - 1, tt), :]
    for k in range(1, CONV_K):
        xc = xc + cw_ref[k:k + 1, :] * xr_s[pl.ds(HALO - 1 + k, tt), :]
    lane = lax.broadcasted_iota(I32, (tt, S5_WIDTH), 1)
    first = lane < S5_WIDTH // 2
    u_ref[...] = u
    gr_ref[...] = gr
    xcf_ref[...] = xc
    ua_ref[...] = jnp.concatenate([jnp.where(first, u, 0.0), jnp.where(first, 0.0, u)], axis=1).astype(BF16)
    xca_ref[...] = jnp.concatenate([jnp.where(first, xc, 0.0), jnp.where(first, 0.0, xc)], axis=1).astype(BF16)


def _inproj0(xall, sc1, sh1, w_in, cw8, cb, n_lat):
    b_, nt, d = xall.shape
    tt = TOK_TILE
    nlt = n_lat // tt
    nti = nt // tt
    hb = tt // HALO
    nh = nt // HALO
    row_idx = lambda b, i: (jnp.where(i < nlt, b, b_), 0, 0)
    kern = functools.partial(_inproj0_kernel, nlt=nlt)
    return pl.pallas_call(
        kern,
        grid=(b_, nti),
        in_specs=[
            pl.BlockSpec((None, HALO, d), lambda b, i: (b, jnp.maximum(i * hb - 1, 0), 0)),
            pl.BlockSpec((None, tt, d), lambda b, i: (b, i, 0)),
            pl.BlockSpec((None, HALO, d), lambda b, i: (b, jnp.minimum(i * hb + hb, nh - 1), 0)),
            pl.BlockSpec((None, 1, d), row_idx),
            pl.BlockSpec((None, 1, d), row_idx),
            pl.BlockSpec(w_in.shape, lambda b, i: (0, 0)),
            pl.BlockSpec((8, RG_WIDTH), lambda b, i: (0, 0)),
            pl.BlockSpec((1, RG_WIDTH), lambda b, i: (0, 0)),
        ],
        out_specs=[
            pl.BlockSpec((None, tt, S5_WIDTH), lambda b, i: (b, i, 0)),
            pl.BlockSpec((tt, 2 * S5_WIDTH), lambda b, i: (i, b)),
            pl.BlockSpec((tt, RG_WIDTH), lambda b, i: (i, b)),
            pl.BlockSpec((tt, 2 * RG_WIDTH), lambda b, i: (i, b)),
            pl.BlockSpec((None, tt, RG_WIDTH), lambda b, i: (b, i, 0)),
        ],
        out_shape=[
            jax.ShapeDtypeStruct((b_, nt, S5_WIDTH), F32),
            jax.ShapeDtypeStruct((nt, b_ * 2 * S5_WIDTH), BF16),
            jax.ShapeDtypeStruct((nt, b_ * RG_WIDTH), F32),
            jax.ShapeDtypeStruct((nt, b_ * 2 * RG_WIDTH), BF16),
            jax.ShapeDtypeStruct((b_, nt, RG_WIDTH), F32),
        ],
        scratch_shapes=[pltpu.VMEM((tt + 2 * HALO, RG_WIDTH), F32)],
        compiler_params=_cp(("parallel", "parallel")),
        name="inproj0",
    )(xall, xall, xall, sc1, sh1, w_in, cw8, cb)


def _scan0_kernel(ua_ref, xca_ref, xcf_ref, bst_ref, cm_ref, ar_ref, ai_ref,
                  wa_ref, wx_ref, ba_ref, bx_ref, sp_ref,
                  ya_ref, hg_ref, s_s, ga_s, gb_s, hr_s, hi_s, hgs_s, *, rows):
    d = pl.program_id(0)
    j = pl.program_id(1)
    tts = SCAN_TILE
    r_tot = rows * tts

    @pl.when(j == 0)
    def _():
        hr_s[...] = jnp.zeros_like(hr_s)
        hi_s[...] = jnp.zeros_like(hi_s)
        hgs_s[...] = jnp.zeros_like(hgs_s)

    s_s[...] = _dot(ua_ref[...], bst_ref[...])

    xa = xca_ref[...]
    pa = _dot(xa, wa_ref[...]).reshape(tts, rows, HALF_RG) + ba_ref[...][None]
    px = _dot(xa, wx_ref[...]).reshape(tts, rows, HALF_RG) + bx_ref[...][None]
    log_a = (-RG_C) * jax.nn.sigmoid(pa) * sp_ref[...][None]
    beta = jnp.sqrt(1.0 - jnp.exp(2.0 * log_a))
    xc = xcf_ref[...].reshape(tts, rows, HALF_RG)
    ga_s[...] = jnp.exp(log_a).reshape(r_tot, HALF_RG)
    gb_s[...] = (beta * jax.nn.sigmoid(px) * xc).reshape(r_tot, HALF_RG)

    ar = ar_ref[...]
    ai = ai_ref[...]

    def step(k, carry):
        hr, hi, hg = carry
        t = jnp.where(d == 0, k, tts - 1 - k)
        r0 = pl.multiple_of(t * rows, rows)
        br = s_s[pl.ds(r0, rows), 0:HALF_STATE]
        bi = s_s[pl.ds(r0, rows), HALF_STATE:2 * HALF_STATE]
        nhr = ar * hr - ai * hi + br
        nhi = ar * hi + ai * hr + bi
        s_s[pl.ds(r0, rows), 0:HALF_STATE] = nhr
        s_s[pl.ds(r0, rows), HALF_STATE:2 * HALF_STATE] = nhi
        nhg = ga_s[pl.ds(r0, rows), :] * hg + gb_s[pl.ds(r0, rows), :]
        gb_s[pl.ds(r0, rows), :] = nhg
        return nhr, nhi, nhg

    hr, hi, hg = lax.fori_loop(0, tts, step, (hr_s[...], hi_s[...], hgs_s[...]))
    hr_s[...] = hr
    hi_s[...] = hi
    hgs_s[...] = hg
    ya_ref[...] = _dot(s_s[...].astype(BF16), cm_ref[...])
    hg_ref[...] = gb_s[...]


def _scan0(ua2, xca2, xcf2, bst, cm, ar, ai, wa, wx, ba, bx, sp, n_lat, rows):
    r_all = ua2.shape[0]
    tts = SCAN_TILE
    rt = rows * tts
    ntile = r_all // rt
    nlat = n_lat // tts
    nctx = ntile - nlat

    def tile(d, j):
        fwd = jnp.where(j < nctx, nlat + j, j - nctx)
        return jnp.where(d == 0, fwd, ntile - 1 - j)

    dsel = lambda d, j: (d, 0, 0)
    kern = functools.partial(_scan0_kernel, rows=rows)
    return pl.pallas_call(
        kern,
        grid=(2, ntile),
        in_specs=[
            pl.BlockSpec((rt, 2 * HALF_RG), lambda d, j: (tile(d, j), 0)),
            pl.BlockSpec((rt, 2 * HALF_RG), lambda d, j: (tile(d, j), 0)),
            pl.BlockSpec((rt, HALF_RG), lambda d, j: (tile(d, j), 0)),
            pl.BlockSpec((None,) + bst.shape[1:], dsel),
            pl.BlockSpec((None,) + cm.shape[1:], dsel),
            pl.BlockSpec((None,) + ar.shape[1:], dsel),
            pl.BlockSpec((None,) + ai.shape[1:], dsel),
            pl.BlockSpec((None,) + wa.shape[1:], dsel),
            pl.BlockSpec((None,) + wx.shape[1:], dsel),
            pl.BlockSpec((None,) + ba.shape[1:], dsel),
            pl.BlockSpec((None,) + bx.shape[1:], dsel),
            pl.BlockSpec((None,) + sp.shape[1:], dsel),
        ],
        out_specs=[
            pl.BlockSpec((None, rt, S5_WIDTH), lambda d, j: (d, tile(d, j), 0)),
            pl.BlockSpec((None, rt, HALF_RG), lambda d, j: (d, tile(d, j), 0)),
        ],
        out_shape=[
            jax.ShapeDtypeStruct((2, r_all, S5_WIDTH), F32),
            jax.ShapeDtypeStruct((2, r_all, HALF_RG), F32),
        ],
        scratch_shapes=[
            pltpu.VMEM((rt, 2 * HALF_STATE), F32),
            pltpu.VMEM((rt, HALF_RG), F32),
            pltpu.VMEM((rt, HALF_RG), F32),
            pltpu.VMEM((rows, HALF_STATE), F32),
            pltpu.VMEM((rows, HALF_STATE), F32),
            pltpu.VMEM((rows, HALF_RG), F32),
        ],
        compiler_params=_cp(("arbitrary", "arbitrary")),
        name="scan0",
    )(ua2, xca2, xcf2, bst, cm, ar, ai, wa, wx, ba, bx, sp)


def _merge0_kernel(ya_ref, hg_ref, u_ref, gr_ref, x_ref, g1_ref, d_ref, wglu_ref, wout_ref, o_ref):
    q = S5_WIDTH // 2
    ya = ya_ref[0] + ya_ref[1]
    y = jnp.concatenate([ya[:, 0:q], ya[:, 3 * q:4 * q]], axis=1) + d_ref[...] * u_ref[...]
    y = _gelu(y)
    y = y * jax.nn.sigmoid(_dot(y.astype(BF16), wglu_ref[...]))
    g = (hg_ref[0] + hg_ref[1]) * _gelu(gr_ref[...])
    m = _dot(jnp.concatenate([y, g], axis=1).astype(BF16), wout_ref[...])
    o_ref[...] = x_ref[...] + g1_ref[...] * m


def _merge0(ya, hg, u, gr, xall, g1, s5d, wglu, wout, n_lat):
    b_, nt, d = xall.shape
    tt = TOK_TILE
    nlt = n_lat // tt
    row_idx = lambda b, i: (jnp.where(i < nlt, b, b_), 0, 0)
    return pl.pallas_call(
        _merge0_kernel,
        grid=(b_, nt // tt),
        in_specs=[
            pl.BlockSpec((2, tt, 2 * S5_WIDTH), lambda b, i: (0, i, b)),
            pl.BlockSpec((2, tt, RG_WIDTH), lambda b, i: (0, i, b)),
            pl.BlockSpec((None, tt, S5_WIDTH), lambda b, i: (b, i, 0)),
            pl.BlockSpec((None, tt, RG_WIDTH), lambda b, i: (b, i, 0)),
            pl.BlockSpec((None, tt, d), lambda b, i: (b, i, 0)),
            pl.BlockSpec((None, 1, d), row_idx),
            pl.BlockSpec((1, S5_WIDTH), lambda b, i: (0, 0)),
            pl.BlockSpec(wglu.shape, lambda b, i: (0, 0)),
            pl.BlockSpec(wout.shape, lambda b, i: (0, 0)),
        ],
        out_specs=pl.BlockSpec((None, tt, d), lambda b, i: (b, i, 0)),
        out_shape=jax.ShapeDtypeStruct((b_, nt, d), F32),
        compiler_params=_cp(("parallel", "parallel")),
        name="merge0",
    )(ya, hg, u, gr, xall, g1, s5d, wglu, wout)


def _route_kernel(x_ref, sc_ref, sh_ref, rwh_ref, rwl_ref, rb_ref, hb_ref, meta_ref, cnt_ref, carry_s):
    first = jnp.logical_and(pl.program_id(0) == 0, pl.program_id(1) == 0)

    @pl.when(first)
    def _():
        carry_s[...] = jnp.zeros_like(carry_s)

    tt = TOK_TILE
    h = _norm_mod(x_ref[...], sc_ref[...], sh_ref[...])
    hh, hl = _split2(h)
    logits = _dot(hh, rwh_ref[...]) + (_dot(hl, rwh_ref[...]) + _dot(hh, rwl_ref[...]))
    lane = lax.broadcasted_iota(I32, (tt, 128), 1)
    valid = lane < N_EXPERTS
    neg = -1e30
    lg = jnp.where(valid, logits, neg)
    mx = jnp.max(lg, axis=-1, keepdims=True)
    ex = jnp.where(valid, jnp.exp(lg - mx), 0.0)
    probs = ex / jnp.sum(ex, axis=-1, keepdims=True)
    sel = probs + rb_ref[...]
    grp = jnp.right_shift(lane, 2)
    best = jnp.zeros((tt, 1), I32)
    bestv = jnp.max(jnp.where(jnp.logical_and(valid, grp == 0), sel, neg), axis=-1, keepdims=True)
    for k in range(1, N_EXPERT_GROUPS):
        gk = jnp.max(jnp.where(jnp.logical_and(valid, grp == k), sel, neg), axis=-1, keepdims=True)
        upd = gk > bestv
        best = jnp.where(upd, k, best)
        bestv = jnp.where(upd, gk, bestv)
    msel = jnp.where(jnp.logical_and(valid, grp == best), sel, neg)
    v1 = jnp.max(msel, axis=-1, keepdims=True)
    i1 = jnp.min(jnp.where(msel == v1, lane, 128), axis=-1, keepdims=True)
    msel2 = jnp.where(lane == i1, neg, msel)
    v2 = jnp.max(msel2, axis=-1, keepdims=True)
    i2 = jnp.min(jnp.where(msel2 == v2, lane, 128), axis=-1, keepdims=True)
    p1 = jnp.sum(jnp.where(lane == i1, probs, 0.0), axis=-1, keepdims=True)
    p2 = jnp.sum(jnp.where(lane == i2, probs, 0.0), axis=-1, keepdims=True)
    den = p1 + p2
    g1 = p1 / den
    g2 = p2 / den
    lo = jnp.minimum(i1, i2) - EXPERTS_PER_GROUP * best
    hi = jnp.maximum(i1, i2) - EXPERTS_PER_GROUP * best
    g_lo = jnp.where(i1 < i2, g1, g2)
    g_hi = jnp.where(i1 < i2, g2, g1)
    pair = jnp.right_shift(lo * (7 - lo), 1) + (hi - lo - 1)
    cls = best * 6 + pair

    onehot = jnp.where(lane == cls, 1.0, 0.0)
    r_i = lax.broadcasted_iota(I32, (tt, tt), 0)
    c_i = lax.broadcasted_iota(I32, (tt, tt), 1)
    before = jnp.where(c_i < r_i, 1.0, 0.0).astype(BF16)
    cum = _dot(before, onehot.astype(BF16))
    rank = jnp.sum(onehot * (cum + carry_s[...]), axis=-1, keepdims=True)
    carry_s[...] = carry_s[...] + jnp.sum(onehot, axis=0, keepdims=True)

    hb_ref[:, 0:D_MODEL] = h
    hb_ref[:, D_MODEL:] = jnp.where(lane == 0, g_lo, jnp.where(lane == 1, g_hi, 0.0))
    meta_ref[...] = jnp.where(lane == 0, cls, jnp.where(lane == 1, rank.astype(I32), 0))
    cnt_ref[...] = jnp.broadcast_to(carry_s[...], (8, 128))


def _route(x, sc2, sh2, rwh, rwl, rb, n_lat):
    b_, nt, d = x.shape
    tt = TOK_TILE
    nlt = n_lat // tt
    nti = nt // tt
    row_idx = lambda b, i: (jnp.where(i < nlt, b, b_), 0, 0)
    flat = lambda b, i: (b * nti + i, 0)
    return pl.pallas_call(
        _route_kernel,
        grid=(b_, nti),
        in_specs=[
            pl.BlockSpec((None, tt, d), lambda b, i: (b, i, 0)),
            pl.BlockSpec((None, 1, d), row_idx),
            pl.BlockSpec((None, 1, d), row_idx),
            pl.BlockSpec((d, 128), lambda b, i: (0, 0)),
            pl.BlockSpec((d, 128), lambda b, i: (0, 0)),
            pl.BlockSpec((1, 128), lambda b, i: (0, 0)),
        ],
        out_specs=[
            pl.BlockSpec((tt, ROW_W), flat),
            pl.BlockSpec((tt, 128), flat),
            pl.BlockSpec((8, 128), lambda b, i: (0, 0)),
        ],
        out_shape=[
            jax.ShapeDtypeStruct((b_ * nt, ROW_W), F32),
            jax.ShapeDtypeStruct((b_ * nt, 128), I32),
            jax.ShapeDtypeStruct((8, 128), F32),
        ],
        scratch_shapes=[pltpu.VMEM((1, 128), F32)],
        compiler_params=_cp(("arbitrary", "arbitrary")),
        name="route",
    )(x, sc2, sh2, rwh, rwl, rb)


def _row_copy(src, dst, src_row, dst_row, sem):
    return pltpu.make_async_copy(src.at[pl.ds(src_row, 1)], dst.at[pl.ds(dst_row, 1)], sem)


def _dispatch_kernel(dest_ref, hb_ref, xb_in, xb_out, sem):
    del xb_in
    tt = TOK_TILE

    def issue(r, carry):
        _row_copy(hb_ref, xb_out, r, dest_ref[0, r], sem).start()
        return carry

    lax.fori_loop(0, tt, issue, 0)
    pltpu.make_async_copy(hb_ref, xb_out.at[pl.ds(0, tt)], sem).wait()


def _dispatch(dest3, hb, n_rows):
    t = hb.shape[0]
    tt = TOK_TILE
    zeros = jnp.zeros((n_rows, ROW_W), F32)
    return pl.pallas_call(
        _dispatch_kernel,
        grid=(t // tt,),
        in_specs=[
            pl.BlockSpec((None, 1, tt), lambda i: (i, 0, 0), memory_space=pltpu.SMEM),
            pl.BlockSpec((tt, ROW_W), lambda i: (i, 0)),
            pl.BlockSpec(memory_space=pl.ANY),
        ],
        out_specs=pl.BlockSpec(memory_space=pl.ANY),
        out_shape=jax.ShapeDtypeStruct((n_rows, ROW_W), F32),
        scratch_shapes=[pltpu.SemaphoreType.DMA(())],
        input_output_aliases={2: 0},
        compiler_params=_cp(("arbitrary",)),
        name="dispatch",
    )(dest3, hb, zeros)


def _expert_kernel(ea_ref, eb_ref, nu_ref, x_ref, w1a, w1b, w3a, w3b, w2a, w2b, o_ref):
    del ea_ref, eb_ref
    j = pl.program_id(0)

    @pl.when(j < nu_ref[0])
    def _():
        x = x_ref[:, 0:D_MODEL].astype(BF16)
        g_lo = x_ref[:, D_MODEL:D_MODEL + 1]
        g_hi = x_ref[:, D_MODEL + 1:D_MODEL + 2]

        def ffn(w1, w3, w2):
            a = _dot(x, w1[...])
            b = _dot(x, w3[...])
            return _dot((_silu(a) * b).astype(BF16), w2[...])

        o_ref[...] = g_lo * ffn(w1a, w3a, w2a) + g_hi * ffn(w1b, w3b, w2b)

    @pl.when(j >= nu_ref[0])
    def _():
        o_ref[...] = jnp.zeros_like(o_ref)


def _experts(blk_ea, blk_eb, nused, xb, w1, w3, w2):
    n_rows = xb.shape[0]
    nb = n_rows // MOE_BLOCK
    d = D_MODEL
    wa = lambda j, ea, eb, nu: (ea[j], 0, 0)
    wb = lambda j, ea, eb, nu: (eb[j], 0, 0)
    wspec = lambda f: pl.BlockSpec((None, d, d), f)
    gs = pltpu.PrefetchScalarGridSpec(
        num_scalar_prefetch=3,
        grid=(nb,),
        in_specs=[pl.BlockSpec((MOE_BLOCK, ROW_W), lambda j, ea, eb, nu: (j, 0)),
                  wspec(wa), wspec(wb), wspec(wa), wspec(wb), wspec(wa), wspec(wb)],
        out_specs=pl.BlockSpec((MOE_BLOCK, d), lambda j, ea, eb, nu: (j, 0)),
    )
    return pl.pallas_call(
        _expert_kernel,
        grid_spec=gs,
        out_shape=jax.ShapeDtypeStruct((n_rows, d), F32),
        compiler_params=_cp(("arbitrary",)),
        name="experts",
    )(blk_ea, blk_eb, nused, xb, w1, w1, w3, w3, w2, w2)


def _combine_kernel(dest_ref, yb_ref, x_ref, g2_ref, nw_ref, o_ref, buf, sem, *, final):
    tt = TOK_TILE

    def issue(r, carry):
        _row_copy(yb_ref, buf, dest_ref[0, r], r, sem).start()
        return carry

    lax.fori_loop(0, tt, issue, 0)
    pltpu.make_async_copy(yb_ref.at[pl.ds(0, tt)], buf, sem).wait()
    x2 = x_ref[...] + g2_ref[...] * buf[...]
    if final:
        ms = jnp.mean(x2 * x2, axis=-1, keepdims=True)
        x2 = x2 * lax.rsqrt(ms + EPS) * nw_ref[...]
    o_ref[...] = x2


def _combine(dest3, yb, x, g2, nw, n_lat, final):
    b_, nt, d = x.shape
    tt = TOK_TILE
    nlt = n_lat // tt
    nti = nt // tt
    row_idx = lambda b, i: (jnp.where(i < nlt, b, b_), 0, 0)
    kern = functools.partial(_combine_kernel, final=final)
    return pl.pallas_call(
        kern,
        grid=(b_, nti),
        in_specs=[
            pl.BlockSpec((None, 1, tt), lambda b, i: (b * nti + i, 0, 0), memory_space=pltpu.SMEM),
            pl.BlockSpec(memory_space=pl.ANY),
            pl.BlockSpec((None, tt, d), lambda b, i: (b, i, 0)),
            pl.BlockSpec((None, 1, d), row_idx),
            pl.BlockSpec((1, d), lambda b, i: (0, 0)),
        ],
        out_specs=pl.BlockSpec((None, tt, d), lambda b, i: (b, i, 0)),
        out_shape=jax.ShapeDtypeStruct((b_, nt, d), F32),
        scratch_shapes=[pltpu.VMEM((tt, d), F32), pltpu.SemaphoreType.DMA(())],
        compiler_params=_cp(("arbitrary", "arbitrary")),
        name="combine",
    )(dest3, yb, x, g2, nw)


def _class_tables():
    ea, eb = [], []
    for g in range(N_EXPERT_GROUPS):
        for lo in range(EXPERTS_PER_GROUP):
            for hi in range(lo + 1, EXPERTS_PER_GROUP):
                ea.append(g * EXPERTS_PER_GROUP + lo)
                eb.append(g * EXPERTS_PER_GROUP + hi)
    return jnp.array(ea, I32), jnp.array(eb, I32)


def _moe(x, sc2, sh2, g2, rwh, rwl, rb, w1, w3, w2, nw, n_lat, final):
    b_, nt, d = x.shape
    t = b_ * nt
    hb, meta, cnt = _route(x, sc2, sh2, rwh, rwl, rb, n_lat)
    counts = cnt[0, :N_CLASSES].astype(I32)
    padded = (counts + MOE_BLOCK - 1) // MOE_BLOCK * MOE_BLOCK
    pend = jnp.cumsum(padded)
    pstart = pend - padded
    dest = jnp.take(pstart, meta[:, 0]) + meta[:, 1]
    nb = t // MOE_BLOCK + N_CLASSES
    blk_cls = jnp.minimum(jnp.searchsorted(pend, jnp.arange(nb, dtype=I32) * MOE_BLOCK, side='right'),
                          N_CLASSES - 1)
    ea, eb = _class_tables()
    nused = (pend[-1] // MOE_BLOCK).reshape(1).astype(I32)
    dest3 = dest.reshape(t // TOK_TILE, 1, TOK_TILE)
    xb = _dispatch(dest3, hb, nb * MOE_BLOCK)
    yb = _experts(jnp.take(ea, blk_cls), jnp.take(eb, blk_cls), nused, xb, w1, w3, w2)
    return _combine(dest3, yb, x, g2, nw, n_lat, final)


def _inproj1_kernel(*refs, latent, nq):
    if latent:
        (xp_ref, xm_ref, xn_ref, sc_ref, sh_ref, wz_ref, wx_ref, wd_ref, cw_ref, cb_ref, db_ref,
         z_ref, xbc_ref, dt_ref, scr) = refs
        q = pl.program_id(1)
        has_prev = (q > 0).astype(F32)
        has_next = (q < nq - 1).astype(F32)
        cur = jnp.concatenate([xm_ref[:, 0:D_MODEL], xm_ref[:, D_MODEL:2 * D_MODEL]], axis=0)
        xe = jnp.concatenate([xp_ref[...], cur, xn_ref[...]], axis=0)
    else:
        (xm_ref, sc_ref, sh_ref, wz_ref, wx_ref, wd_ref, cw_ref, cb_ref, db_ref,
         z_ref, xbc_ref, dt_ref, scr) = refs
        has_prev = 0.0
        has_next = 0.0
        pad = jnp.zeros((HALO, D_MODEL), F32)
        xe = jnp.concatenate([pad, xm_ref[...], pad], axis=0)
    tt = TOK_TILE
    h = _norm_mod(xe, sc_ref[...], sh_ref[...]).astype(BF16)
    hc = h[HALO:HALO + tt]
    z_ref[...] = _dot(hc, wz_ref[...]).astype(BF16)
    row = lax.broadcasted_iota(I32, (tt + 2 * HALO, 1), 0)
    keep = jnp.where(row < HALO, has_prev, jnp.where(row >= tt + HALO, has_next, 1.0))
    pw = 1024
    for pc in range(M2_CONV_DIM // pw):
        sl = slice(pc * pw, (pc + 1) * pw)
        scr[...] = _dot(h, wx_ref[:, sl]) * keep
        acc = cb_ref[:, sl] + cw_ref[0:1, sl] * scr[pl.ds(HALO - 1, tt), :]
        for k in range(1, CONV_K):
            acc = acc + cw_ref[k:k + 1, sl] * scr[pl.ds(HALO - 1 + k, tt), :]
        xbc_ref[:, sl] = _silu(acc).astype(BF16)
    lane = lax.broadcasted_iota(I32, (tt, 128), 1)
    dt = _softplus(_dot(hc, wd_ref[...]) + db_ref[...])
    dt_ref[...] = jnp.where(lane < 2 * M2_HEADS, dt, 0.0)


def _inproj1(xall, sc1, sh1, wz, wx, wd, cw8, cb, db, n_lat, latent):
    b_, nt, d = xall.shape
    tt = TOK_TILE
    if latent:
        n = n_lat
        rows = n_lat // GRID_W
        nq = GRID_W // 2
        xv = xall.reshape(b_, nt // GRID_W, GRID_W * d)
        rb = rows // HALO
        x_specs = [
            pl.BlockSpec((None, HALO, d), lambda b, q: (b, rb - 1, jnp.maximum(2 * q - 1, 0))),
            pl.BlockSpec((None, rows, 2 * d), lambda b, q: (b, 0, q)),
            pl.BlockSpec((None, HALO, d), lambda b, q: (b, 0, jnp.minimum(2 * q + 2, GRID_W - 1))),
        ]
        x_args = [xv, xv, xv]
        row_idx = lambda b, q: (b, 0, 0)
    else:
        n = nt - n_lat
        nq = n // tt
        off = n_lat // tt
        x_specs = [pl.BlockSpec((None, tt, d), lambda b, q: (b, off + q, 0))]
        x_args = [xall]
        row_idx = lambda b, q: (b_, 0, 0)
    const = lambda b, q: (0, 0)
    kern = functools.partial(_inproj1_kernel, latent=latent, nq=nq)
    return pl.pallas_call(
        kern,
        grid=(b_, nq),
        in_specs=x_specs + [
            pl.BlockSpec((None, 1, d), row_idx),
            pl.BlockSpec((None, 1, d), row_idx),
            pl.BlockSpec(wz.shape, const),
            pl.BlockSpec(wx.shape, const),
            pl.BlockSpec(wd.shape, const),
            pl.BlockSpec(cw8.shape, const),
            pl.BlockSpec(cb.shape, const),
            pl.BlockSpec(db.shape, const),
        ],
        out_specs=[
            pl.BlockSpec((None, tt, M2_INNER), lambda b, q: (b, q, 0)),
            pl.BlockSpec((None, tt, M2_CONV_DIM), lambda b, q: (b, q, 0)),
            pl.BlockSpec((None, tt, 128), lambda b, q: (b, q, 0)),
        ],
        out_shape=[
            jax.ShapeDtypeStruct((b_, n, M2_INNER), BF16),
            jax.ShapeDtypeStruct((b_, n, M2_CONV_DIM), BF16),
            jax.ShapeDtypeStruct((b_, n, 128), F32),
        ],
        scratch_shapes=[pltpu.VMEM((tt + 2 * HALO, 1024), F32)],
        compiler_params=_cp(("parallel", "parallel")),
        name="inproj1_lat" if latent else "inproj1_ctx",
    )(*x_args, sc1, sh1, wz, wx, wd, cw8, cb, db)


def _ssd_kernel(*refs, reverse, need_y, lane0):
    if need_y:
        xbc_ref, dt_ref, arow_ref, e_ref, s0_ref, y_ref, sf_ref, s_s = refs
    else:
        xbc_ref, dt_ref, arow_ref, e_ref, s0_ref, sf_ref, s_s = refs
    c = pl.program_id(1)
    nc = pl.num_programs(1)
    q = M2_CHUNK

    @pl.when(c == 0)
    def _():
        s_s[...] = s0_ref[...]

    dt = dt_ref[...]
    a = dt * arow_ref[...]
    row = lax.broadcasted_iota(I32, (q, q), 0)
    col = lax.broadcasted_iota(I32, (q, q), 1)
    incl = (col >= row) if reverse else (col <= row)
    incl_t = (row >= col) if reverse else (row <= col)
    lt = jnp.where(incl, 1.0, 0.0).astype(BF16)
    lt_t = jnp.where(incl_t, 1.0, 0.0).astype(BF16)
    a1, a2, a3 = _split3(a)
    cum = _dot(lt, a1) + (_dot(lt, a2) + _dot(lt, a3))
    b1, b2, b3 = _split3(a.T)
    cum_t = _dot(b1, lt_t) + (_dot(b2, lt_t) + _dot(b3, lt_t))
    atot = cum[0:1, :] if reverse else cum[q - 1:q, :]
    w = dt * jnp.exp(atot - cum)
    dt_t = dt.T
    eh, el = _split2(jnp.broadcast_to(jnp.exp(atot), (8, 128)))
    atx = _dot(eh, e_ref[...]) + _dot(el, e_ref[...])

    for g in range(M2_GROUPS):
        bg = xbc_ref[:, M2_INNER + M2_STATE * g:M2_INNER + M2_STATE * (g + 1)]
        gw = M2_HPG * M2_HEAD_DIM
        xg = xbc_ref[:, gw * g:gw * (g + 1)]
        sg = s_s[g]
        if need_y:
            cg = xbc_ref[:, M2_INNER + M2_GROUPS * M2_STATE + M2_STATE * g:
                         M2_INNER + M2_GROUPS * M2_STATE + M2_STATE * (g + 1)]
            cb = lax.dot_general(cg, bg, (((1,), (1,)), ((), ())), preferred_element_type=F32)
            cg32 = cg.astype(F32)
            sgb = sg.astype(BF16)
            y_parts = []
        xw_parts = []
        for r in range(M2_HPG):
            hl = lane0 + M2_HPG * g + r
            ps = slice(M2_HEAD_DIM * r, M2_HEAD_DIM * (r + 1))
            if need_y:
                colb = jnp.broadcast_to(cum[:, hl:hl + 1], (q, q))
                rowb = jnp.broadcast_to(cum_t[hl:hl + 1, :], (q, q))
                lmat = jnp.exp(jnp.where(incl, colb - rowb, -1e30))
                gm = (cb * lmat * dt_t[hl:hl + 1, :]).astype(BF16)
                cd = (jnp.exp(colb) * cg32).astype(BF16)
                lhs = jnp.concatenate([gm, cd], axis=1)
                rhs = jnp.concatenate([xg[:, ps], sgb[:, ps]], axis=0)
                y_parts.append(_dot(lhs, rhs))
            xw_parts.append(xg[:, ps].astype(F32) * w[:, hl:hl + 1])
        xw = jnp.concatenate(xw_parts, axis=1).astype(BF16)
        snew = lax.dot_general(bg, xw, (((0,), (0,)), ((), ())), preferred_element_type=F32)
        s_s[g] = atx[0:1, gw * g:gw * (g + 1)] * sg + snew
        if need_y:
            y_ref[:, gw * g:gw * (g + 1)] = jnp.concatenate(y_parts, axis=1)

    @pl.when(c == nc - 1)
    def _():
        sf_ref[...] = s_s[...]


def _ssd(xbc, dt, arow, emat, s0, reverse, need_y):
    b_, n, _ = xbc.shape
    q = M2_CHUNK
    nc = n // q
    cidx = (lambda c: nc - 1 - c) if reverse else (lambda c: c)
    sshape = (M2_GROUPS, M2_STATE, M2_HPG * M2_HEAD_DIM)
    kern = functools.partial(_ssd_kernel, reverse=reverse, need_y=need_y, lane0=M2_HEADS if reverse else 0)
    out_specs = [pl.BlockSpec((None,) + sshape, lambda b, c: (b, 0, 0, 0))]
    out_shape = [jax.ShapeDtypeStruct((b_,) + sshape, F32)]
    if need_y:
        out_specs = [pl.BlockSpec((None, q, M2_INNER), lambda b, c: (b, cidx(c), 0))] + out_specs
        out_shape = [jax.ShapeDtypeStruct((b_, n, M2_INNER), F32)] + out_shape
    return pl.pallas_call(
        kern,
        grid=(b_, nc),
        in_specs=[
            pl.BlockSpec((None, q, M2_CONV_DIM), lambda b, c: (b, cidx(c), 0)),
            pl.BlockSpec((None, q, 128), lambda b, c: (b, cidx(c), 0)),
            pl.BlockSpec((1, 128), lambda b, c: (0, 0)),
            pl.BlockSpec(emat.shape, lambda b, c: (0, 0)),
            pl.BlockSpec((None,) + sshape, lambda b, c: (b, 0, 0, 0)),
        ],
        out_specs=out_specs,
        out_shape=out_shape,
        scratch_shapes=[pltpu.VMEM(sshape, F32)],
        compiler_params=_cp(("arbitrary", "arbitrary")),
        name=("ssd_lat" if need_y else "ssd_ctx") + ("_bwd" if reverse else "_fwd"),
    )(xbc, dt, arow, emat, s0)


def _fin1_kernel(yf_ref, yb_ref, z_ref, xs_ref, x_ref, g1_ref, dx_ref, nw_ref, wout_ref, o_ref):
    y = yf_ref[...] + yb_ref[...] + dx_ref[...] * xs_ref[...].astype(F32)
    y = y * _silu(z_ref[...].astype(F32))
    ms = jnp.mean(y * y, axis=-1, keepdims=True)
    yn = (y * lax.rsqrt(ms + EPS)) * nw_ref[...]
    m = _dot(yn.astype(BF16), wout_ref[...])
    rows = m.shape[0] // 2
    o_ref[...] = x_ref[...] + g1_ref[...] * jnp.concatenate([m[0:rows], m[rows:]], axis=1)


def _fin1(yf, yb, z, xbc, xall, g1x2, dx, nw, wout, n_lat):
    b_, nt, d = xall.shape
    tt = TOK_TILE
    rows = n_lat // GRID_W
    xv = xall.reshape(b_, nt // GRID_W, GRID_W * d)
    const = lambda b, q: (0, 0)
    out = pl.pallas_call(
        _fin1_kernel,
        grid=(b_, GRID_W // 2),
        in_specs=[
            pl.BlockSpec((None, tt, M2_INNER), lambda b, q: (b, q, 0)),
            pl.BlockSpec((None, tt, M2_INNER), lambda b, q: (b, q, 0)),
            pl.BlockSpec((None, tt, M2_INNER), lambda b, q: (b, q, 0)),
            pl.BlockSpec((None, tt, M2_INNER), lambda b, q: (b, q, 0)),
            pl.BlockSpec((None, rows, 2 * d), lambda b, q: (b, 0, q)),
            pl.BlockSpec((None, 1, 2 * d), lambda b, q: (b, 0, 0)),
            pl.BlockSpec((1, M2_INNER), const),
            pl.BlockSpec((1, M2_INNER), const),
            pl.BlockSpec(wout.shape, const),
        ],
        out_specs=pl.BlockSpec((None, rows, 2 * d), lambda b, q: (b, 0, q)),
        out_shape=jax.ShapeDtypeStruct((b_, rows, GRID_W * d), F32),
        compiler_params=_cp(("parallel", "parallel")),
        name="fin1",
    )(yf, yb, z, xbc, xv, g1x2, dx, nw, wout)
    return out.reshape(b_, n_lat, d)


def _s5_params(lam_re, lam_im, log_dt, b_re, b_im, c_re, c_im, rows):
    lam_re = lam_re.astype(F32)
    lam_im = lam_im.astype(F32)
    dt = jnp.exp(log_dt.astype(F32))[..., None]
    mag = jnp.exp(lam_re * dt)
    ar = mag * jnp.cos(lam_im * dt)
    ai = mag * jnp.sin(lam_im * dt)
    den = lam_re * lam_re + lam_im * lam_im
    fr = ((ar - 1.0) * lam_re + ai * lam_im) / den
    fi = (ai * lam_re - (ar - 1.0) * lam_im) / den
    bbr = fr[..., None] * b_re - fi[..., None] * b_im
    bbi = fr[..., None] * b_im + fi[..., None] * b_re
    hg = HALF_GROUPS
    eye = jnp.eye(hg, dtype=F32)

    def in_mat(bb):
        bb = bb.reshape(2, 2, hg, S5_STATE, S5_GROUP)
        m = jnp.einsum('dhgpk,gj->dhgkjp', bb, eye)
        return m.reshape(2, 2 * hg * S5_GROUP, hg * S5_STATE)

    bst = jnp.concatenate([in_mat(bbr), in_mat(bbi)], axis=-1).astype(BF16)

    def out_mat(cc):
        cc = cc.astype(F32).reshape(2, 2, hg, S5_GROUP, S5_STATE)
        m = jnp.einsum('dhgkp,gj->dgphjk', cc, eye)
        return m.reshape(2, hg * S5_STATE, 2 * hg * S5_GROUP)

    cm = jnp.concatenate([out_mat(c_re), -out_mat(c_im)], axis=1).astype(BF16)
    half_rows = lambda v: jnp.tile(v.reshape(2, 2, HALF_STATE), (1, rows // 2, 1))
    return bst, cm, half_rows(ar), half_rows(ai)


def _rg_params(wa, ba, wx, bx, lam, rows):
    hh = RG_HEADS // 2
    eye = jnp.eye(hh, dtype=F32)

    def stack(w):
        w = w.astype(F32).reshape(2, 2, hh, RG_HEAD_DIM, RG_HEAD_DIM)
        m = jnp.einsum('dhgij,gk->dhgikj', w, eye)
        return m.reshape(2, RG_WIDTH, HALF_RG).astype(BF16)

    half_rows = lambda v: jnp.tile(v.astype(F32).reshape(2, 2, HALF_RG), (1, rows // 2, 1))
    sp = jax.nn.softplus(-lam.astype(F32))
    return stack(wa), stack(wx), half_rows(ba), half_rows(bx), half_rows(sp)


def kernel(x, c, ctx, c_ctx, w_mod, b_mod, norm_mix, norm_ffn, norm_final, router_w, router_b, exp_w1, exp_w3, exp_w2, ab_w_in, ab_w_out, s5_lam_re, s5_lam_im, s5_log_dt, s5_b_re, s5_b_im, s5_c_re, s5_c_im, s5_d, s5_w_glu, rg_conv_w, rg_conv_b, rg_wa, rg_ba, rg_wx, rg_bx, rg_lam, m2_w_in, m2_conv_w, m2_conv_b, m2_dt_bias, m2_a_log, m2_d, m2_norm, m2_w_out):
    b_, n_lat, d = x.shape
    n_ctx = ctx.shape[1]
    rows = 2 * b_
    assert d == D_MODEL and rows % 8 == 0
    assert n_lat % (GRID_W * HALO) == 0 and n_lat // GRID_W == M2_CHUNK
    assert n_lat % TOK_TILE == 0 and n_ctx % TOK_TILE == 0 and n_ctx % GRID_W == 0
    nt = n_lat + n_ctx

    xall = jnp.concatenate([x, ctx], axis=1)
    c8 = jnp.concatenate([c, c_ctx[None], jnp.zeros((7 - b_, d), F32)], axis=0)
    mods = _modulation(c8, w_mod, b_mod)
    nrow = b_ + 1

    def mod_rows(layer):
        m = mods[layer, :nrow].reshape(nrow, 6, 1, d)
        sh1, sc1, g1, sh2, sc2, g2 = (m[:, k] for k in range(6))
        return (norm_mix[layer] * (1.0 + sc1), sh1, g1, norm_ffn[layer] * (1.0 + sc2), sh2, g2)

    rwh, rwl = _split2(jnp.pad(router_w.astype(F32), ((0, 0), (0, 128 - N_EXPERTS))))
    rb = jnp.pad(router_b.astype(F32), (0, 128 - N_EXPERTS)).reshape(1, 128)
    one_row = jnp.ones((1, d), F32)

    sc1, sh1, g1, sc2, sh2, g2 = mod_rows(0)
    cw8 = jnp.pad(rg_conv_w[0].astype(F32), ((0, 8 - CONV_K), (0, 0)))
    u, ua, xcf, xca, gr = _inproj0(xall, sc1, sh1, ab_w_in[0].astype(BF16), cw8,
                                   rg_conv_b[0].astype(F32).reshape(1, RG_WIDTH), n_lat)
    bst, cm, ar, ai = _s5_params(s5_lam_re[0], s5_lam_im[0], s5_log_dt[0], s5_b_re[0], s5_b_im[0],
                                 s5_c_re[0], s5_c_im[0], rows)
    wa, wx, ba, bx, sp = _rg_params(rg_wa[0], rg_ba[0], rg_wx[0], rg_bx[0], rg_lam[0], rows)
    ya, hg = _scan0(ua.reshape(nt * rows, S5_WIDTH), xca.reshape(nt * rows, RG_WIDTH),
                    xcf.reshape(nt * rows, HALF_RG), bst, cm, ar, ai, wa, wx, ba, bx, sp, n_lat, rows)
    x1 = _merge0(ya.reshape(2, nt, b_ * 2 * S5_WIDTH), hg.reshape(2, nt, b_ * RG_WIDTH), u, gr, xall, g1,
                 s5_d[0].astype(F32).reshape(1, S5_WIDTH), s5_w_glu[0].astype(BF16),
                 ab_w_out[0].astype(BF16), n_lat)
    x2 = _moe(x1, sc2, sh2, g2, rwh, rwl, rb, exp_w1[0].astype(BF16), exp_w3[0].astype(BF16),
              exp_w2[0].astype(BF16), one_row, n_lat, final=False)

    sc1, sh1, g1, sc2, sh2, g2 = mod_rows(1)
    w_in = m2_w_in[0]
    wz = w_in[:, :M2_INNER].astype(BF16)
    wxbc = w_in[:, M2_INNER:M2_INNER + M2_CONV_DIM].astype(BF16)
    wd = jnp.pad(w_in[:, M2_INNER + M2_CONV_DIM:], ((0, 0), (0, 128 - 2 * M2_HEADS))).astype(BF16)
    cw8 = jnp.pad(m2_conv_w[0].astype(F32), ((0, 8 - CONV_K), (0, 0)))
    cb = m2_conv_b[0].astype(F32).reshape(1, M2_CONV_DIM)
    db = jnp.pad(m2_dt_bias[0].astype(F32).reshape(2 * M2_HEADS), (0, 128 - 2 * M2_HEADS)).reshape(1, 128)
    z_l, xbc_l, dt_l = _inproj1(x2, sc1, sh1, wz, wxbc, wd, cw8, cb, db, n_lat, latent=True)
    _, xbc_c, dt_c = _inproj1(x2, sc1, sh1, wz, wxbc, wd, cw8, cb, db, n_lat, latent=False)
    a_neg = -jnp.exp(m2_a_log[0].astype(F32))
    lanes = jnp.arange(128)
    heads = jnp.arange(M2_INNER) // M2_HEAD_DIM
    s0 = jnp.zeros((b_, M2_GROUPS, M2_STATE, M2_HPG * M2_HEAD_DIM), F32)
    ys = []
    for dirn in range(2):
        rev = dirn == 1
        arow = jnp.zeros((128,), F32).at[dirn * M2_HEADS:(dirn + 1) * M2_HEADS].set(a_neg[dirn]).reshape(1, 128)
        emat = (lanes[:, None] == heads[None, :] + dirn * M2_HEADS).astype(BF16)
        (st,) = _ssd(xbc_c, dt_c, arow, emat, s0, rev, need_y=False)
        y, _ = _ssd(xbc_l, dt_l, arow, emat, st, rev, need_y=True)
        ys.append(y)
    g1x2 = jnp.concatenate([g1[:b_], g1[:b_]], axis=-1)
    dx = jnp.repeat(m2_d[0].astype(F32), M2_HEAD_DIM).reshape(1, M2_INNER)
    x3 = _fin1(ys[0], ys[1], z_l, xbc_l, x2, g1x2, dx, m2_norm[0].astype(F32).reshape(1, M2_INNER),
               m2_w_out[0].astype(BF16), n_lat)
    return _moe(x3, sc2, sh2, g2, rwh, rwl, rb, exp_w1[1].astype(BF16), exp_w3[1].astype(BF16),
                exp_w2[1].astype(BF16), norm_final.astype(F32).reshape(1, d), n_lat, final=True)
```

```python
import functools
import math

import jax
import jax.numpy as jnp
from jax import lax
from jax.experimental import pallas as pl
from jax.experimental.pallas import tpu as pltpu

F32 = jnp.float32
BF16 = jnp.bfloat16
I32 = jnp.int32

D_MODEL = 1024
GRID_W = 64
EPS = 1e-6

S5_WIDTH = 512
S5_GROUP = 16
S5_GROUPS = 32
S5_STATE = 64
RG_WIDTH = 512
RG_HEADS = 8
RG_HEAD_DIM = 64
RG_C = 8.0
CONV_K = 4
HALF_GROUPS = S5_GROUPS // 2
HALF_STATE = HALF_GROUPS * S5_STATE
HALF_RG = RG_WIDTH // 2

M2_INNER = 2048
M2_HEAD_DIM = 64
M2_HEADS = 32
M2_GROUPS = 8
M2_HPG = 4
M2_STATE = 128
M2_CHUNK = 128
M2_CONV_DIM = M2_INNER + 2 * M2_GROUPS * M2_STATE

N_EXPERTS = 16
N_EXPERT_GROUPS = 4
EXPERTS_PER_GROUP = 4
N_CLASSES = 24
MOE_BLOCK = 256
ROW_W = D_MODEL + 128

TOK_TILE = 256
HALO = 8
SCAN_TILE = 128
VMEM_LIMIT = 52 * 1024 * 1024


def _cp(sem, vmem=VMEM_LIMIT):
    return pltpu.CompilerParams(dimension_semantics=sem, vmem_limit_bytes=vmem)


def _dot(a, b):
    return jnp.dot(a, b, preferred_element_type=F32)


def _split2(a):
    hi = a.astype(BF16)
    lo = (a - hi.astype(F32)).astype(BF16)
    return hi, lo


def _split3(a):
    a1 = a.astype(BF16)
    r1 = a - a1.astype(F32)
    a2 = r1.astype(BF16)
    a3 = (r1 - a2.astype(F32)).astype(BF16)
    return a1, a2, a3


def _dot3(a, b):
    ah, al = _split2(a)
    bh, bl = _split2(b)
    return _dot(ah, bh) + (_dot(al, bh) + _dot(ah, bl))


def _norm_mod(x, scale, shift):
    ms = jnp.mean(x * x, axis=-1, keepdims=True)
    return (x * lax.rsqrt(ms + EPS)) * scale + shift


def _silu(x):
    return x * jax.nn.sigmoid(x)


def _gelu(x):
    c = math.sqrt(2.0 / math.pi)
    return 0.5 * x * (1.0 + jnp.tanh(c * (x + 0.044715 * (x * x * x))))


def _softplus(x):
    return jnp.maximum(x, 0.0) + jnp.log(1.0 + jnp.exp(-jnp.abs(x)))


def _mod_kernel(c_ref, w_ref, b_ref, o_ref):
    c = c_ref[...]
    o_ref[...] = _dot3(_silu(c), w_ref[...]) + b_ref[...]


def _modulation(c8, w_mod, b_mod):
    depth, d, n = w_mod.shape
    tn = 1536
    return pl.pallas_call(
        _mod_kernel,
        grid=(depth, n // tn),
        in_specs=[pl.BlockSpec((8, d), lambda l, j: (0, 0)),
                  pl.BlockSpec((None, d, tn), lambda l, j: (l, 0, j)),
                  pl.BlockSpec((None, 1, tn), lambda l, j: (l, 0, j))],
        out_specs=pl.BlockSpec((None, 8, tn), lambda l, j: (l, 0, j)),
        out_shape=jax.ShapeDtypeStruct((depth, 8, n), F32),
        compiler_params=_cp(("parallel", "parallel")),
        name="modulation",
    )(c8, w_mod, b_mod.reshape(depth, 1, n))


def _inproj0_kernel(xp_ref, xc_ref, xn_ref, sc_ref, sh_ref, w_ref, cw_ref, cb_ref,
                    u_ref, ua_ref, xcf_ref, xca_ref, gr_ref, xr_s, *, nlt):
    i = pl.program_id(1)
    tt = TOK_TILE
    has_prev = jnp.logical_and(i > 0, i < nlt).astype(F32)
    has_next = (i < nlt - 1).astype(F32)
    xe = jnp.concatenate([xp_ref[...], xc_ref[...], xn_ref[...]], axis=0)
    h = _norm_mod(xe, sc_ref[...], sh_ref[...]).astype(BF16)
    p = _dot(h, w_ref[...])
    u = p[HALO:HALO + tt, 0:S5_WIDTH]
    gr = p[HALO:HALO + tt, S5_WIDTH + RG_WIDTH:]
    row = lax.broadcasted_iota(I32, (tt + 2 * HALO, 1), 0)
    keep = jnp.where(row < HALO, has_prev, jnp.where(row >= tt + HALO, has_next, 1.0))
    xr_s[...] = p[:, S5_WIDTH:S5_WIDTH + RG_WIDTH] * keep
    xc = cb_ref[...] + cw_ref[0:1, :] * xr_s[pl.ds(HALO - 1, tt), :]
    for k in range(1, CONV_K):
        xc = xc + cw_ref[k:k + 1, :] * xr_s[pl.ds(HALO - 1 + k, tt), :]
    lane = lax.broadcasted_iota(I32, (tt, S5_WIDTH), 1)
    first = lane < S5_WIDTH // 2
    u_ref[...] = u
    gr_ref[...] = gr
    xcf_ref[...] = xc
    ua_ref[...] = jnp.concatenate([jnp.where(first, u, 0.0), jnp.where(first, 0.0, u)], axis=1).astype(BF16)
    xca_ref[...] = jnp.concatenate([jnp.where(first, xc, 0.0), jnp.where(first, 0.0, xc)], axis=1).astype(BF16)


def _inproj0(xall, sc1, sh1, w_in, cw8, cb, n_lat):
    b_, nt, d = xall.shape
    tt = TOK_TILE
    nlt = n_lat // tt
    nti = nt // tt
    hb = tt // HALO
    nh = nt // HALO
    row_idx = lambda b, i: (jnp.where(i < nlt, b, b_), 0, 0)
    kern = functools.partial(_inproj0_kernel, nlt=nlt)
    return pl.pallas_call(
        kern,
        grid=(b_, nti),
        in_specs=[
            pl.BlockSpec((None, HALO, d), lambda b, i: (b, jnp.maximum(i * hb - 1, 0), 0)),
            pl.BlockSpec((None, tt, d), lambda b, i: (b, i, 0)),
            pl.BlockSpec((None, HALO, d), lambda b, i: (b, jnp.minimum(i * hb + hb, nh - 1), 0)),
            pl.BlockSpec((None, 1, d), row_idx),
            pl.BlockSpec((None, 1, d), row_idx),
            pl.BlockSpec(w_in.shape, lambda b, i: (0, 0)),
            pl.BlockSpec((8, RG_WIDTH), lambda b, i: (0, 0)),
            pl.BlockSpec((1, RG_WIDTH), lambda b, i: (0, 0)),
        ],
        out_specs=[
            pl.BlockSpec((None, tt, S5_WIDTH), lambda b, i: (b, i, 0)),
            pl.BlockSpec((tt, 2 * S5_WIDTH), lambda b, i: (i, b)),
            pl.BlockSpec((tt, RG_WIDTH), lambda b, i: (i, b)),
            pl.BlockSpec((tt, 2 * RG_WIDTH), lambda b, i: (i, b)),
            pl.BlockSpec((None, tt, RG_WIDTH), lambda b, i: (b, i, 0)),
        ],
        out_shape=[
            jax.ShapeDtypeStruct((b_, nt, S5_WIDTH), F32),
            jax.ShapeDtypeStruct((nt, b_ * 2 * S5_WIDTH), BF16),
            jax.ShapeDtypeStruct((nt, b_ * RG_WIDTH), F32),
            jax.ShapeDtypeStruct((nt, b_ * 2 * RG_WIDTH), BF16),
            jax.ShapeDtypeStruct((b_, nt, RG_WIDTH), F32),
        ],
        scratch_shapes=[pltpu.VMEM((tt + 2 * HALO, RG_WIDTH), F32)],
        compiler_params=_cp(("parallel", "parallel")),
        name="inproj0",
    )(xall, xall, xall, sc1, sh1, w_in, cw8, cb)


def _scan0_kernel(ua_ref, xca_ref, xcf_ref, bst_ref, cm_ref, ar_ref, ai_ref,
                  wa_ref, wx_ref, ba_ref, bx_ref, sp_ref,
                  ya_ref, hg_ref, s_s, ga_s, gb_s, hr_s, hi_s, hgs_s, *, rows):
    d = pl.program_id(0)
    j = pl.program_id(1)
    tts = SCAN_TILE
    r_tot = rows * tts

    @pl.when(j == 0)
    def _():
        hr_s[...] = jnp.zeros_like(hr_s)
        hi_s[...] = jnp.zeros_like(hi_s)
        hgs_s[...] = jnp.zeros_like(hgs_s)

    s_s[...] = _dot(ua_ref[...], bst_ref[...])

    xa = xca_ref[...]
    pa = _dot(xa, wa_ref[...]).reshape(tts, rows, HALF_RG) + ba_ref[...][None]
    px = _dot(xa, wx_ref[...]).reshape(tts, rows, HALF_RG) + bx_ref[...][None]
    log_a = (-RG_C) * jax.nn.sigmoid(pa) * sp_ref[...][None]
    beta = jnp.sqrt(1.0 - jnp.exp(2.0 * log_a))
    xc = xcf_ref[...].reshape(tts, rows, HALF_RG)
    ga_s[...] = jnp.exp(log_a).reshape(r_tot, HALF_RG)
    gb_s[...] = (beta * jax.nn.sigmoid(px) * xc).reshape(r_tot, HALF_RG)

    ar = ar_ref[...]
    ai = ai_ref[...]

    def step(k, carry):
        hr, hi, hg = carry
        t = jnp.where(d == 0, k, tts - 1 - k)
        r0 = pl.multiple_of(t * rows, rows)
        br = s_s[pl.ds(r0, rows), 0:HALF_STATE]
        bi = s_s[pl.ds(r0, rows), HALF_STATE:2 * HALF_STATE]
        nhr = ar * hr - ai * hi + br
        nhi = ar * hi + ai * hr + bi
        s_s[pl.ds(r0, rows), 0:HALF_STATE] = nhr
        s_s[pl.ds(r0, rows), HALF_STATE:2 * HALF_STATE] = nhi
        nhg = ga_s[pl.ds(r0, rows), :] * hg + gb_s[pl.ds(r0, rows), :]
        gb_s[pl.ds(r0, rows), :] = nhg
        return nhr, nhi, nhg

    hr, hi, hg = lax.fori_loop(0, tts, step, (hr_s[...], hi_s[...], hgs_s[...]))
    hr_s[...] = hr
    hi_s[...] = hi
    hgs_s[...] = hg
    ya_ref[...] = _dot(s_s[...].astype(BF16), cm_ref[...])
    hg_ref[...] = gb_s[...]


def _scan0(ua2, xca2, xcf2, bst, cm, ar, ai, wa, wx, ba, bx, sp, n_lat, rows):
    r_all = ua2.shape[0]
    tts = SCAN_TILE
    rt = rows * tts
    ntile = r_all // rt
    nlat = n_lat // tts
    nctx = ntile - nlat

    def tile(d, j):
        fwd = jnp.where(j < nctx, nlat + j, j - nctx)
        return jnp.where(d == 0, fwd, ntile - 1 - j)

    dsel = lambda d, j: (d, 0, 0)
    kern = functools.partial(_scan0_kernel, rows=rows)
    return pl.pallas_call(
        kern,
        grid=(2, ntile),
        in_specs=[
            pl.BlockSpec((rt, 2 * HALF_RG), lambda d, j: (tile(d, j), 0)),
            pl.BlockSpec((rt, 2 * HALF_RG), lambda d, j: (tile(d, j), 0)),
            pl.BlockSpec((rt, HALF_RG), lambda d, j: (tile(d, j), 0)),
            pl.BlockSpec((None,) + bst.shape[1:], dsel),
            pl.BlockSpec((None,) + cm.shape[1:], dsel),
            pl.BlockSpec((None,) + ar.shape[1:], dsel),
            pl.BlockSpec((None,) + ai.shape[1:], dsel),
            pl.BlockSpec((None,) + wa.shape[1:], dsel),
            pl.BlockSpec((None,) + wx.shape[1:], dsel),
            pl.BlockSpec((None,) + ba.shape[1:], dsel),
            pl.BlockSpec((None,) + bx.shape[1:], dsel),
            pl.BlockSpec((None,) + sp.shape[1:], dsel),
        ],
        out_specs=[
            pl.BlockSpec((None, rt, S5_WIDTH), lambda d, j: (d, tile(d, j), 0)),
            pl.BlockSpec((None, rt, HALF_RG), lambda d, j: (d, tile(d, j), 0)),
        ],
        out_shape=[
            jax.ShapeDtypeStruct((2, r_all, S5_WIDTH), F32),
            jax.ShapeDtypeStruct((2, r_all, HALF_RG), F32),
        ],
        scratch_shapes=[
            pltpu.VMEM((rt, 2 * HALF_STATE), F32),
            pltpu.VMEM((rt, HALF_RG), F32),
            pltpu.VMEM((rt, HALF_RG), F32),
            pltpu.VMEM((rows, HALF_STATE), F32),
            pltpu.VMEM((rows, HALF_STATE), F32),
            pltpu.VMEM((rows, HALF_RG), F32),
        ],
        compiler_params=_cp(("arbitrary", "arbitrary")),
        name="scan0",
    )(ua2, xca2, xcf2, bst, cm, ar, ai, wa, wx, ba, bx, sp)


def _merge0_kernel(ya_ref, hg_ref, u_ref, gr_ref, x_ref, g1_ref, d_ref, wglu_ref, wout_ref, o_ref):
    q = S5_WIDTH // 2
    ya = ya_ref[0] + ya_ref[1]
    y = jnp.concatenate([ya[:, 0:q], ya[:, 3 * q:4 * q]], axis=1) + d_ref[...] * u_ref[...]
    y = _gelu(y)
    y = y * jax.nn.sigmoid(_dot(y.astype(BF16), wglu_ref[...]))
    g = (hg_ref[0] + hg_ref[1]) * _gelu(gr_ref[...])
    m = _dot(jnp.concatenate([y, g], axis=1).astype(BF16), wout_ref[...])
    o_ref[...] = x_ref[...] + g1_ref[...] * m


def _merge0(ya, hg, u, gr, xall, g1, s5d, wglu, wout, n_lat):
    b_, nt, d = xall.shape
    tt = TOK_TILE
    nlt = n_lat // tt
    row_idx = lambda b, i: (jnp.where(i < nlt, b, b_), 0, 0)
    return pl.pallas_call(
        _merge0_kernel,
        grid=(b_, nt // tt),
        in_specs=[
            pl.BlockSpec((2, tt, 2 * S5_WIDTH), lambda b, i: (0, i, b)),
            pl.BlockSpec((2, tt, RG_WIDTH), lambda b, i: (0, i, b)),
            pl.BlockSpec((None, tt, S5_WIDTH), lambda b, i: (b, i, 0)),
            pl.BlockSpec((None, tt, RG_WIDTH), lambda b, i: (b, i, 0)),
            pl.BlockSpec((None, tt, d), lambda b, i: (b, i, 0)),
            pl.BlockSpec((None, 1, d), row_idx),
            pl.BlockSpec((1, S5_WIDTH), lambda b, i: (0, 0)),
            pl.BlockSpec(wglu.shape, lambda b, i: (0, 0)),
            pl.BlockSpec(wout.shape, lambda b, i: (0, 0)),
        ],
        out_specs=pl.BlockSpec((None, tt, d), lambda b, i: (b, i, 0)),
        out_shape=jax.ShapeDtypeStruct((b_, nt, d), F32),
        compiler_params=_cp(("parallel", "parallel")),
        name="merge0",
    )(ya, hg, u, gr, xall, g1, s5d, wglu, wout)


def _route_kernel(x_ref, sc_ref, sh_ref, rwh_ref, rwl_ref, rb_ref, hb_ref, meta_ref, cnt_ref, carry_s):
    first = jnp.logical_and(pl.program_id(0) == 0, pl.program_id(1) == 0)

    @pl.when(first)
    def _():
        carry_s[...] = jnp.zeros_like(carry_s)

    tt = TOK_TILE
    h = _norm_mod(x_ref[...], sc_ref[...], sh_ref[...])
    hh, hl = _split2(h)
    logits = _dot(hh, rwh_ref[...]) + (_dot(hl, rwh_ref[...]) + _dot(hh, rwl_ref[...]))
    lane = lax.broadcasted_iota(I32, (tt, 128), 1)
    valid = lane < N_EXPERTS
    neg = -1e30
    lg = jnp.where(valid, logits, neg)
    mx = jnp.max(lg, axis=-1, keepdims=True)
    ex = jnp.where(valid, jnp.exp(lg - mx), 0.0)
    probs = ex / jnp.sum(ex, axis=-1, keepdims=True)
    sel = probs + rb_ref[...]
    grp = jnp.right_shift(lane, 2)
    best = jnp.zeros((tt, 1), I32)
    bestv = jnp.max(jnp.where(jnp.logical_and(valid, grp == 0), sel, neg), axis=-1, keepdims=True)
    for k in range(1, N_EXPERT_GROUPS):
        gk = jnp.max(jnp.where(jnp.logical_and(valid, grp == k), sel, neg), axis=-1, keepdims=True)
        upd = gk > bestv
        best = jnp.where(upd, k, best)
        bestv = jnp.where(upd, gk, bestv)
    msel = jnp.where(jnp.logical_and(valid, grp == best), sel, neg)
    v1 = jnp.max(msel, axis=-1, keepdims=True)
    i1 = jnp.min(jnp.where(msel == v1, lane, 128), axis=-1, keepdims=True)
    msel2 = jnp.where(lane == i1, neg, msel)
    v2 = jnp.max(msel2, axis=-1, keepdims=True)
    i2 = jnp.min(jnp.where(msel2 == v2, lane, 128), axis=-1, keepdims=True)
    p1 = jnp.sum(jnp.where(lane == i1, probs, 0.0), axis=-1, keepdims=True)
    p2 = jnp.sum(jnp.where(lane == i2, probs, 0.0), axis=-1, keepdims=True)
    den = p1 + p2
    g1 = p1 / den
    g2 = p2 / den
    lo = jnp.minimum(i1, i2) - EXPERTS_PER_GROUP * best
    hi = jnp.maximum(i1, i2) - EXPERTS_PER_GROUP * best
    g_lo = jnp.where(i1 < i2, g1, g2)
    g_hi = jnp.where(i1 < i2, g2, g1)
    pair = jnp.right_shift(lo * (7 - lo), 1) + (hi - lo - 1)
    cls = best * 6 + pair

    onehot = jnp.where(lane == cls, 1.0, 0.0)
    r_i = lax.broadcasted_iota(I32, (tt, tt), 0)
    c_i = lax.broadcasted_iota(I32, (tt, tt), 1)
    before = jnp.where(c_i < r_i, 1.0, 0.0).astype(BF16)
    cum = _dot(before, onehot.astype(BF16))
    rank = jnp.sum(onehot * (cum + carry_s[...]), axis=-1, keepdims=True)
    carry_s[...] = carry_s[...] + jnp.sum(onehot, axis=0, keepdims=True)

    hb_ref[:, 0:D_MODEL] = h
    hb_ref[:, D_MODEL:] = jnp.where(lane == 0, g_lo, jnp.where(lane == 1, g_hi, 0.0))
    meta = jnp.where(lane == 0, cls.astype(F32), jnp.where(lane == 1, rank, 0.0))
    meta_ref[...] = meta.T[0:8, :].astype(I32)
    cnt_ref[...] = jnp.broadcast_to(carry_s[...], (8, 128))


def _route(x, sc2, sh2, rwh, rwl, rb, n_lat):
    b_, nt, d = x.shape
    tt = TOK_TILE
    nlt = n_lat // tt
    nti = nt // tt
    row_idx = lambda b, i: (jnp.where(i < nlt, b, b_), 0, 0)
    flat = lambda b, i: (b * nti + i, 0)
    return pl.pallas_call(
        _route_kernel,
        grid=(b_, nti),
        in_specs=[
            pl.BlockSpec((None, tt, d), lambda b, i: (b, i, 0)),
            pl.BlockSpec((None, 1, d), row_idx),
            pl.BlockSpec((None, 1, d), row_idx),
            pl.BlockSpec((d, 128), lambda b, i: (0, 0)),
            pl.BlockSpec((d, 128), lambda b, i: (0, 0)),
            pl.BlockSpec((1, 128), lambda b, i: (0, 0)),
        ],
        out_specs=[
            pl.BlockSpec((tt, ROW_W), flat),
            pl.BlockSpec((None, 8, tt), lambda b, i: (b * nti + i, 0, 0)),
            pl.BlockSpec((8, 128), lambda b, i: (0, 0)),
        ],
        out_shape=[
            jax.ShapeDtypeStruct((b_ * nt, ROW_W), F32),
            jax.ShapeDtypeStruct((b_ * nti, 8, tt), I32),
            jax.ShapeDtypeStruct((8, 128), F32),
        ],
        scratch_shapes=[pltpu.VMEM((1, 128), F32)],
        compiler_params=_cp(("arbitrary", "arbitrary")),
        name="route",
    )(x, sc2, sh2, rwh, rwl, rb)


def _row_copy(src, dst, src_row, dst_row, sem):
    return pltpu.make_async_copy(src.at[pl.ds(src_row, 1)], dst.at[pl.ds(dst_row, 1)], sem)


def _dest_row(ps_ref, meta_ref, r):
    return ps_ref[meta_ref[0, r]] + meta_ref[1, r]


def _dispatch_kernel(ps_ref, meta_ref, hb_ref, xb_in, xb_out, sem):
    del xb_in
    tt = TOK_TILE

    def issue(r, carry):
        _row_copy(hb_ref, xb_out, r, _dest_row(ps_ref, meta_ref, r), sem).start()
        return carry

    lax.fori_loop(0, tt, issue, 0)
    pltpu.make_async_copy(hb_ref, xb_out.at[pl.ds(0, tt)], sem).wait()


def _dispatch(pstart, meta, hb, n_rows):
    t = hb.shape[0]
    tt = TOK_TILE
    zeros = jnp.zeros((n_rows, ROW_W), F32)
    gs = pltpu.PrefetchScalarGridSpec(
        num_scalar_prefetch=1,
        grid=(t // tt,),
        in_specs=[
            pl.BlockSpec((None, 8, tt), lambda i, ps: (i, 0, 0), memory_space=pltpu.SMEM),
            pl.BlockSpec((tt, ROW_W), lambda i, ps: (i, 0)),
            pl.BlockSpec(memory_space=pl.ANY),
        ],
        out_specs=pl.BlockSpec(memory_space=pl.ANY),
        scratch_shapes=[pltpu.SemaphoreType.DMA(())],
    )
    return pl.pallas_call(
        _dispatch_kernel,
        grid_spec=gs,
        out_shape=jax.ShapeDtypeStruct((n_rows, ROW_W), F32),
        input_output_aliases={3: 0},
        compiler_params=_cp(("arbitrary",)),
        name="dispatch",
    )(pstart, meta, hb, zeros)


def _expert_kernel(ea_ref, eb_ref, nu_ref, x_ref, w1a, w1b, w3a, w3b, w2a, w2b, o_ref):
    del ea_ref, eb_ref
    j = pl.program_id(0)

    @pl.when(j < nu_ref[0])
    def _():
        x = x_ref[:, 0:D_MODEL].astype(BF16)
        g_lo = x_ref[:, D_MODEL:D_MODEL + 1]
        g_hi = x_ref[:, D_MODEL + 1:D_MODEL + 2]

        def ffn(w1, w3, w2):
            a = _dot(x, w1[...])
            b = _dot(x, w3[...])
            return _dot((_silu(a) * b).astype(BF16), w2[...])

        o_ref[...] = g_lo * ffn(w1a, w3a, w2a) + g_hi * ffn(w1b, w3b, w2b)

    @pl.when(j >= nu_ref[0])
    def _():
        o_ref[...] = jnp.zeros_like(o_ref)


def _experts(blk_ea, blk_eb, nused, xb, w1, w3, w2):
    n_rows = xb.shape[0]
    nb = n_rows // MOE_BLOCK
    d = D_MODEL
    wa = lambda j, ea, eb, nu: (ea[j], 0, 0)
    wb = lambda j, ea, eb, nu: (eb[j], 0, 0)
    wspec = lambda f: pl.BlockSpec((None, d, d), f)
    gs = pltpu.PrefetchScalarGridSpec(
        num_scalar_prefetch=3,
        grid=(nb,),
        in_specs=[pl.BlockSpec((MOE_BLOCK, ROW_W), lambda j, ea, eb, nu: (j, 0)),
                  wspec(wa), wspec(wb), wspec(wa), wspec(wb), wspec(wa), wspec(wb)],
        out_specs=pl.BlockSpec((MOE_BLOCK, d), lambda j, ea, eb, nu: (j, 0)),
    )
    return pl.pallas_call(
        _expert_kernel,
        grid_spec=gs,
        out_shape=jax.ShapeDtypeStruct((n_rows, d), F32),
        compiler_params=_cp(("arbitrary",)),
        name="experts",
    )(blk_ea, blk_eb, nused, xb, w1, w1, w3, w3, w2, w2)


def _combine_kernel(ps_ref, meta_ref, yb_ref, x_ref, g2_ref, nw_ref, o_ref, buf, sem, *, final):
    tt = TOK_TILE

    def issue(r, carry):
        _row_copy(yb_ref, buf, _dest_row(ps_ref, meta_ref, r), r, sem).start()
        return carry

    lax.fori_loop(0, tt, issue, 0)
    pltpu.make_async_copy(yb_ref.at[pl.ds(0, tt)], buf, sem).wait()
    x2 = x_ref[...] + g2_ref[...] * buf[...]
    if final:
        ms = jnp.mean(x2 * x2, axis=-1, keepdims=True)
        x2 = x2 * lax.rsqrt(ms + EPS) * nw_ref[...]
    o_ref[...] = x2


def _combine(pstart, meta, yb, x, g2, nw, n_lat, final):
    b_, nt, d = x.shape
    tt = TOK_TILE
    nlt = n_lat // tt
    nti = nt // tt
    row_idx = lambda b, i, ps: (jnp.where(i < nlt, b, b_), 0, 0)
    kern = functools.partial(_combine_kernel, final=final)
    gs = pltpu.PrefetchScalarGridSpec(
        num_scalar_prefetch=1,
        grid=(b_, nti),
        in_specs=[
            pl.BlockSpec((None, 8, tt), lambda b, i, ps: (b * nti + i, 0, 0), memory_space=pltpu.SMEM),
            pl.BlockSpec(memory_space=pl.ANY),
            pl.BlockSpec((None, tt, d), lambda b, i, ps: (b, i, 0)),
            pl.BlockSpec((None, 1, d), row_idx),
            pl.BlockSpec((1, d), lambda b, i, ps: (0, 0)),
        ],
        out_specs=pl.BlockSpec((None, tt, d), lambda b, i, ps: (b, i, 0)),
        scratch_shapes=[pltpu.VMEM((tt, d), F32), pltpu.SemaphoreType.DMA(())],
    )
    return pl.pallas_call(
        kern,
        grid_spec=gs,
        out_shape=jax.ShapeDtypeStruct((b_, nt, d), F32),
        compiler_params=_cp(("arbitrary", "arbitrary")),
        name="combine",
    )(pstart, meta, yb, x, g2, nw)


def _class_tables():
    ea, eb = [], []
    for g in range(N_EXPERT_GROUPS):
        for lo in range(EXPERTS_PER_GROUP):
            for hi in range(lo + 1, EXPERTS_PER_GROUP):
                ea.append(g * EXPERTS_PER_GROUP + lo)
                eb.append(g * EXPERTS_PER_GROUP + hi)
    return jnp.array(ea, I32), jnp.array(eb, I32)


def _moe(x, sc2, sh2, g2, rwh, rwl, rb, w1, w3, w2, nw, n_lat, final):
    b_, nt, d = x.shape
    t = b_ * nt
    hb, meta, cnt = _route(x, sc2, sh2, rwh, rwl, rb, n_lat)
    counts = cnt[0, :N_CLASSES].astype(I32)
    padded = (counts + MOE_BLOCK - 1) // MOE_BLOCK * MOE_BLOCK
    pend = jnp.cumsum(padded)
    pstart = pend - padded
    nb = t // MOE_BLOCK + N_CLASSES
    blk_start = jnp.arange(nb, dtype=I32)[:, None] * MOE_BLOCK
    blk_cls = jnp.minimum(jnp.sum((pend[None, :] <= blk_start).astype(I32), axis=1), N_CLASSES - 1)
    ea, eb = _class_tables()
    nused = (pend[-1] // MOE_BLOCK).reshape(1).astype(I32)
    xb = _dispatch(pstart, meta, hb, nb * MOE_BLOCK)
    yb = _experts(jnp.take(ea, blk_cls), jnp.take(eb, blk_cls), nused, xb, w1, w3, w2)
    return _combine(pstart, meta, yb, x, g2, nw, n_lat, final)


def _inproj1_kernel(*refs, latent, nq):
    if latent:
        (xp_ref, xm_ref, xn_ref, sc_ref, sh_ref, wz_ref, wx_ref, wd_ref, cw_ref, cb_ref, db_ref,
         z_ref, xbc_ref, dt_ref, scr) = refs
        q = pl.program_id(1)
        has_prev = (q > 0).astype(F32)
        has_next = (q < nq - 1).astype(F32)
        xe = jnp.concatenate([xp_ref[...], xm_ref[0], xm_ref[1], xn_ref[...]], axis=0)
    else:
        (xm_ref, sc_ref, sh_ref, wz_ref, wx_ref, wd_ref, cw_ref, cb_ref, db_ref,
         z_ref, xbc_ref, dt_ref, scr) = refs
        has_prev = 0.0
        has_next = 0.0
        pad = jnp.zeros((HALO, D_MODEL), F32)
        xe = jnp.concatenate([pad, xm_ref[...], pad], axis=0)
    tt = TOK_TILE
    h = _norm_mod(xe, sc_ref[...], sh_ref[...]).astype(BF16)
    hc = h[HALO:HALO + tt]
    z_ref[...] = _dot(hc, wz_ref[...]).astype(BF16)
    row = lax.broadcasted_iota(I32, (tt + 2 * HALO, 1), 0)
    keep = jnp.where(row < HALO, has_prev, jnp.where(row >= tt + HALO, has_next, 1.0))
    pw = 1024
    for pc in range(M2_CONV_DIM // pw):
        sl = slice(pc * pw, (pc + 1) * pw)
        scr[...] = _dot(h, wx_ref[:, sl]) * keep
        acc = cb_ref[:, sl] + cw_ref[0:1, sl] * scr[pl.ds(HALO - 1, tt), :]
        for k in range(1, CONV_K):
            acc = acc + cw_ref[k:k + 1, sl] * scr[pl.ds(HALO - 1 + k, tt), :]
        xbc_ref[:, sl] = _silu(acc).astype(BF16)
    lane = lax.broadcasted_iota(I32, (tt, 128), 1)
    dt = _softplus(_dot(hc, wd_ref[...]) + db_ref[...])
    dt_ref[...] = jnp.where(lane < 2 * M2_HEADS, dt, 0.0)


def _inproj1(xall, xcol, sc1, sh1, wz, wx, wd, cw8, cb, db, n_lat, latent):
    b_, nt, d = xall.shape
    tt = TOK_TILE
    if latent:
        n = n_lat
        rows = n_lat // GRID_W
        nq = GRID_W // 2
        rb = rows // HALO
        x_specs = [
            pl.BlockSpec((None, None, HALO, d), lambda b, q: (b, jnp.maximum(2 * q - 1, 0), rb - 1, 0)),
            pl.BlockSpec((None, 2, rows, d), lambda b, q: (b, q, 0, 0)),
            pl.BlockSpec((None, None, HALO, d), lambda b, q: (b, jnp.minimum(2 * q + 2, GRID_W - 1), 0, 0)),
        ]
        x_args = [xcol, xcol, xcol]
        row_idx = lambda b, q: (b, 0, 0)
    else:
        n = nt - n_lat
        nq = n // tt
        off = n_lat // tt
        x_specs = [pl.BlockSpec((None, tt, d), lambda b, q: (b, off + q, 0))]
        x_args = [xall]
        row_idx = lambda b, q: (b_, 0, 0)
    const = lambda b, q: (0, 0)
    kern = functools.partial(_inproj1_kernel, latent=latent, nq=nq)
    return pl.pallas_call(
        kern,
        grid=(b_, nq),
        in_specs=x_specs + [
            pl.BlockSpec((None, 1, d), row_idx),
            pl.BlockSpec((None, 1, d), row_idx),
            pl.BlockSpec(wz.shape, const),
            pl.BlockSpec(wx.shape, const),
            pl.BlockSpec(wd.shape, const),
            pl.BlockSpec(cw8.shape, const),
            pl.BlockSpec(cb.shape, const),
            pl.BlockSpec(db.shape, const),
        ],
        out_specs=[
            pl.BlockSpec((None, tt, M2_INNER), lambda b, q: (b, q, 0)),
            pl.BlockSpec((None, tt, M2_CONV_DIM), lambda b, q: (b, q, 0)),
            pl.BlockSpec((None, tt, 128), lambda b, q: (b, q, 0)),
        ],
        out_shape=[
            jax.ShapeDtypeStruct((b_, n, M2_INNER), BF16),
            jax.ShapeDtypeStruct((b_, n, M2_CONV_DIM), BF16),
            jax.ShapeDtypeStruct((b_, n, 128), F32),
        ],
        scratch_shapes=[pltpu.VMEM((tt + 2 * HALO, 1024), F32)],
        compiler_params=_cp(("parallel", "parallel")),
        name="inproj1_lat" if latent else "inproj1_ctx",
    )(*x_args, sc1, sh1, wz, wx, wd, cw8, cb, db)


def _ssd_kernel(*refs, reverse, need_y, lane0):
    if need_y:
        xbc_ref, dt_ref, arow_ref, e_ref, s0_ref, y_ref, sf_ref, s_s = refs
    else:
        xbc_ref, dt_ref, arow_ref, e_ref, s0_ref, sf_ref, s_s = refs
    c = pl.program_id(1)
    nc = pl.num_programs(1)
    q = M2_CHUNK

    @pl.when(c == 0)
    def _():
        s_s[...] = s0_ref[...]

    dt = dt_ref[...]
    a = dt * arow_ref[...]
    row = lax.broadcasted_iota(I32, (q, q), 0)
    col = lax.broadcasted_iota(I32, (q, q), 1)
    incl = (col >= row) if reverse else (col <= row)
    incl_t = (row >= col) if reverse else (row <= col)
    lt = jnp.where(incl, 1.0, 0.0).astype(BF16)
    lt_t = jnp.where(incl_t, 1.0, 0.0).astype(BF16)
    a1, a2, a3 = _split3(a)
    cum = _dot(lt, a1) + (_dot(lt, a2) + _dot(lt, a3))
    b1, b2, b3 = _split3(a.T)
    cum_t = _dot(b1, lt_t) + (_dot(b2, lt_t) + _dot(b3, lt_t))
    atot = cum[0:1, :] if reverse else cum[q - 1:q, :]
    w = dt * jnp.exp(atot - cum)
    dt_t = dt.T
    eh, el = _split2(jnp.broadcast_to(jnp.exp(atot), (8, 128)))
    atx = _dot(eh, e_ref[...]) + _dot(el, e_ref[...])

    for g in range(M2_GROUPS):
        bg = xbc_ref[:, M2_INNER + M2_STATE * g:M2_INNER + M2_STATE * (g + 1)]
        gw = M2_HPG * M2_HEAD_DIM
        xg = xbc_ref[:, gw * g:gw * (g + 1)]
        sg = s_s[g]
        if need_y:
            cg = xbc_ref[:, M2_INNER + M2_GROUPS * M2_STATE + M2_STATE * g:
                         M2_INNER + M2_GROUPS * M2_STATE + M2_STATE * (g + 1)]
            cb = lax.dot_general(cg, bg, (((1,), (1,)), ((), ())), preferred_element_type=F32)
            cg32 = cg.astype(F32)
            sgb = sg.astype(BF16)
            y_parts = []
        xw_parts = []
        for r in range(M2_HPG):
            hl = lane0 + M2_HPG * g + r
            ps = slice(M2_HEAD_DIM * r, M2_HEAD_DIM * (r + 1))
            if need_y:
                colb = jnp.broadcast_to(cum[:, hl:hl + 1], (q, q))
                rowb = jnp.broadcast_to(cum_t[hl:hl + 1, :], (q, q))
                lmat = jnp.exp(jnp.where(incl, colb - rowb, -1e30))
                gm = (cb * lmat * dt_t[hl:hl + 1, :]).astype(BF16)
                cd = (jnp.exp(colb) * cg32).astype(BF16)
                lhs = jnp.concatenate([gm, cd], axis=1)
                rhs = jnp.concatenate([xg[:, ps], sgb[:, ps]], axis=0)
                y_parts.append(_dot(lhs, rhs))
            xw_parts.append(xg[:, ps].astype(F32) * w[:, hl:hl + 1])
        xw = jnp.concatenate(xw_parts, axis=1).astype(BF16)
        snew = lax.dot_general(bg, xw, (((0,), (0,)), ((), ())), preferred_element_type=F32)
        s_s[g] = atx[0:1, gw * g:gw * (g + 1)] * sg + snew
        if need_y:
            y_ref[:, gw * g:gw * (g + 1)] = jnp.concatenate(y_parts, axis=1)

    @pl.when(c == nc - 1)
    def _():
        sf_ref[...] = s_s[...]


def _ssd(xbc, dt, arow, emat, s0, reverse, need_y):
    b_, n, _ = xbc.shape
    q = M2_CHUNK
    nc = n // q
    cidx = (lambda c: nc - 1 - c) if reverse else (lambda c: c)
    sshape = (M2_GROUPS, M2_STATE, M2_HPG * M2_HEAD_DIM)
    kern = functools.partial(_ssd_kernel, reverse=reverse, need_y=need_y, lane0=M2_HEADS if reverse else 0)
    out_specs = [pl.BlockSpec((None,) + sshape, lambda b, c: (b, 0, 0, 0))]
    out_shape = [jax.ShapeDtypeStruct((b_,) + sshape, F32)]
    if need_y:
        out_specs = [pl.BlockSpec((None, q, M2_INNER), lambda b, c: (b, cidx(c), 0))] + out_specs
        out_shape = [jax.ShapeDtypeStruct((b_, n, M2_INNER), F32)] + out_shape
    return pl.pallas_call(
        kern,
        grid=(b_, nc),
        in_specs=[
            pl.BlockSpec((None, q, M2_CONV_DIM), lambda b, c: (b, cidx(c), 0)),
            pl.BlockSpec((None, q, 128), lambda b, c: (b, cidx(c), 0)),
            pl.BlockSpec((1, 128), lambda b, c: (0, 0)),
            pl.BlockSpec(emat.shape, lambda b, c: (0, 0)),
            pl.BlockSpec((None,) + sshape, lambda b, c: (b, 0, 0, 0)),
        ],
        out_specs=out_specs,
        out_shape=out_shape,
        scratch_shapes=[pltpu.VMEM(sshape, F32)],
        compiler_params=_cp(("arbitrary", "arbitrary")),
        name=("ssd_lat" if need_y else "ssd_ctx") + ("_bwd" if reverse else "_fwd"),
    )(xbc, dt, arow, emat, s0)


def _fin1_kernel(yf_ref, yb_ref, z_ref, xs_ref, x_ref, g1_ref, dx_ref, nw_ref, wout_ref, o_ref):
    y = yf_ref[...] + yb_ref[...] + dx_ref[...] * xs_ref[...].astype(F32)
    y = y * _silu(z_ref[...].astype(F32))
    ms = jnp.mean(y * y, axis=-1, keepdims=True)
    yn = (y * lax.rsqrt(ms + EPS)) * nw_ref[...]
    m = _dot(yn.astype(BF16), wout_ref[...])
    o_ref[...] = x_ref[...] + g1_ref[...] * m


def _fin1(yf, yb, z, xbc, xflat, g1, dx, nw, wout):
    b_, n, d = xflat.shape
    tt = TOK_TILE
    const = lambda b, q: (0, 0)
    tok = lambda w: pl.BlockSpec((None, tt, w), lambda b, q: (b, q, 0))
    return pl.pallas_call(
        _fin1_kernel,
        grid=(b_, n // tt),
        in_specs=[
            tok(M2_INNER), tok(M2_INNER), tok(M2_INNER), tok(M2_INNER), tok(d),
            pl.BlockSpec((None, 1, d), lambda b, q: (b, 0, 0)),
            pl.BlockSpec((1, M2_INNER), const),
            pl.BlockSpec((1, M2_INNER), const),
            pl.BlockSpec(wout.shape, const),
        ],
        out_specs=tok(d),
        out_shape=jax.ShapeDtypeStruct((b_, n, d), F32),
        compiler_params=_cp(("parallel", "parallel")),
        name="fin1",
    )(yf, yb, z, xbc, xflat, g1, dx, nw, wout)


def _s5_params(lam_re, lam_im, log_dt, b_re, b_im, c_re, c_im, rows):
    lam_re = lam_re.astype(F32)
    lam_im = lam_im.astype(F32)
    dt = jnp.exp(log_dt.astype(F32))[..., None]
    mag = jnp.exp(lam_re * dt)
    ar = mag * jnp.cos(lam_im * dt)
    ai = mag * jnp.sin(lam_im * dt)
    den = lam_re * lam_re + lam_im * lam_im
    fr = ((ar - 1.0) * lam_re + ai * lam_im) / den
    fi = (ai * lam_re - (ar - 1.0) * lam_im) / den
    bbr = fr[..., None] * b_re - fi[..., None] * b_im
    bbi = fr[..., None] * b_im + fi[..., None] * b_re
    hg = HALF_GROUPS
    eye = jnp.eye(hg, dtype=F32)

    def in_mat(bb):
        bb = bb.reshape(2, 2, hg, S5_STATE, S5_GROUP)
        m = jnp.einsum('dhgpk,gj->dhgkjp', bb, eye)
        return m.reshape(2, 2 * hg * S5_GROUP, hg * S5_STATE)

    bst = jnp.concatenate([in_mat(bbr), in_mat(bbi)], axis=-1).astype(BF16)

    def out_mat(cc):
        cc = cc.astype(F32).reshape(2, 2, hg, S5_GROUP, S5_STATE)
        m = jnp.einsum('dhgkp,gj->dgphjk', cc, eye)
        return m.reshape(2, hg * S5_STATE, 2 * hg * S5_GROUP)

    cm = jnp.concatenate([out_mat(c_re), -out_mat(c_im)], axis=1).astype(BF16)
    half_rows = lambda v: jnp.tile(v.reshape(2, 2, HALF_STATE), (1, rows // 2, 1))
    return bst, cm, half_rows(ar), half_rows(ai)


def _rg_params(wa, ba, wx, bx, lam, rows):
    hh = RG_HEADS // 2
    eye = jnp.eye(hh, dtype=F32)

    def stack(w):
        w = w.astype(F32).reshape(2, 2, hh, RG_HEAD_DIM, RG_HEAD_DIM)
        m = jnp.einsum('dhgij,gk->dhgikj', w, eye)
        return m.reshape(2, RG_WIDTH, HALF_RG).astype(BF16)

    half_rows = lambda v: jnp.tile(v.astype(F32).reshape(2, 2, HALF_RG), (1, rows // 2, 1))
    sp = jax.nn.softplus(-lam.astype(F32))
    return stack(wa), stack(wx), half_rows(ba), half_rows(bx), half_rows(sp)


def kernel(x, c, ctx, c_ctx, w_mod, b_mod, norm_mix, norm_ffn, norm_final, router_w, router_b, exp_w1, exp_w3, exp_w2, ab_w_in, ab_w_out, s5_lam_re, s5_lam_im, s5_log_dt, s5_b_re, s5_b_im, s5_c_re, s5_c_im, s5_d, s5_w_glu, rg_conv_w, rg_conv_b, rg_wa, rg_ba, rg_wx, rg_bx, rg_lam, m2_w_in, m2_conv_w, m2_conv_b, m2_dt_bias, m2_a_log, m2_d, m2_norm, m2_w_out):
    b_, n_lat, d = x.shape
    n_ctx = ctx.shape[1]
    rows = 2 * b_
    assert d == D_MODEL and rows % 8 == 0
    assert n_lat % (GRID_W * HALO) == 0 and n_lat // GRID_W == M2_CHUNK
    assert n_lat % TOK_TILE == 0 and n_ctx % TOK_TILE == 0 and n_ctx % GRID_W == 0
    nt = n_lat + n_ctx

    xall = jnp.concatenate([x, ctx], axis=1)
    c8 = jnp.concatenate([c, c_ctx[None], jnp.zeros((7 - b_, d), F32)], axis=0)
    mods = _modulation(c8, w_mod, b_mod)
    nrow = b_ + 1

    def mod_rows(layer):
        m = mods[layer, :nrow].reshape(nrow, 6, 1, d)
        sh1, sc1, g1, sh2, sc2, g2 = (m[:, k] for k in range(6))
        return (norm_mix[layer] * (1.0 + sc1), sh1, g1, norm_ffn[layer] * (1.0 + sc2), sh2, g2)

    rwh, rwl = _split2(jnp.pad(router_w.astype(F32), ((0, 0), (0, 128 - N_EXPERTS))))
    rb = jnp.pad(router_b.astype(F32), (0, 128 - N_EXPERTS)).reshape(1, 128)
    one_row = jnp.ones((1, d), F32)

    sc1, sh1, g1, sc2, sh2, g2 = mod_rows(0)
    cw8 = jnp.pad(rg_conv_w[0].astype(F32), ((0, 8 - CONV_K), (0, 0)))
    u, ua, xcf, xca, gr = _inproj0(xall, sc1, sh1, ab_w_in[0].astype(BF16), cw8,
                                   rg_conv_b[0].astype(F32).reshape(1, RG_WIDTH), n_lat)
    bst, cm, ar, ai = _s5_params(s5_lam_re[0], s5_lam_im[0], s5_log_dt[0], s5_b_re[0], s5_b_im[0],
                                 s5_c_re[0], s5_c_im[0], rows)
    wa, wx, ba, bx, sp = _rg_params(rg_wa[0], rg_ba[0], rg_wx[0], rg_bx[0], rg_lam[0], rows)
    ya, hg = _scan0(ua.reshape(nt * rows, S5_WIDTH), xca.reshape(nt * rows, RG_WIDTH),
                    xcf.reshape(nt * rows, HALF_RG), bst, cm, ar, ai, wa, wx, ba, bx, sp, n_lat, rows)
    x1 = _merge0(ya.reshape(2, nt, b_ * 2 * S5_WIDTH), hg.reshape(2, nt, b_ * RG_WIDTH), u, gr, xall, g1,
                 s5_d[0].astype(F32).reshape(1, S5_WIDTH), s5_w_glu[0].astype(BF16),
                 ab_w_out[0].astype(BF16), n_lat)
    x2 = _moe(x1, sc2, sh2, g2, rwh, rwl, rb, exp_w1[0].astype(BF16), exp_w3[0].astype(BF16),
              exp_w2[0].astype(BF16), one_row, n_lat, final=False)

    sc1, sh1, g1, sc2, sh2, g2 = mod_rows(1)
    w_in = m2_w_in[0]
    wz = w_in[:, :M2_INNER].astype(BF16)
    wxbc = w_in[:, M2_INNER:M2_INNER + M2_CONV_DIM].astype(BF16)
    wd = jnp.pad(w_in[:, M2_INNER + M2_CONV_DIM:], ((0, 0), (0, 128 - 2 * M2_HEADS))).astype(BF16)
    cw8 = jnp.pad(m2_conv_w[0].astype(F32), ((0, 8 - CONV_K), (0, 0)))
    cb = m2_conv_b[0].astype(F32).reshape(1, M2_CONV_DIM)
    db = jnp.pad(m2_dt_bias[0].astype(F32).reshape(2 * M2_HEADS), (0, 128 - 2 * M2_HEADS)).reshape(1, 128)
    grows = n_lat // GRID_W
    x2v = x2[:, :n_lat].reshape(b_, grows, GRID_W, d).transpose(0, 2, 1, 3)
    z_l, xbc_l, dt_l = _inproj1(x2, x2v, sc1, sh1, wz, wxbc, wd, cw8, cb, db, n_lat, latent=True)
    _, xbc_c, dt_c = _inproj1(x2, x2v, sc1, sh1, wz, wxbc, wd, cw8, cb, db, n_lat, latent=False)
    a_neg = -jnp.exp(m2_a_log[0].astype(F32))
    lanes = jnp.arange(128)
    heads = jnp.arange(M2_INNER) // M2_HEAD_DIM
    s0 = jnp.zeros((b_, M2_GROUPS, M2_STATE, M2_HPG * M2_HEAD_DIM), F32)
    ys = []
    for dirn in range(2):
        rev = dirn == 1
        arow = jnp.zeros((128,), F32).at[dirn * M2_HEADS:(dirn + 1) * M2_HEADS].set(a_neg[dirn]).reshape(1, 128)
        emat = (lanes[:, None] == heads[None, :] + dirn * M2_HEADS).astype(BF16)
        (st,) = _ssd(xbc_c, dt_c, arow, emat, s0, rev, need_y=False)
        y, _ = _ssd(xbc_l, dt_l, arow, emat, st, rev, need_y=True)
        ys.append(y)
    dx = jnp.repeat(m2_d[0].astype(F32), M2_HEAD_DIM).reshape(1, M2_INNER)
    x3 = _fin1(ys[0], ys[1], z_l, xbc_l, x2v.reshape(b_, n_lat, d), g1, dx,
               m2_norm[0].astype(F32).reshape(1, M2_INNER), m2_w_out[0].astype(BF16))
    out = _moe(x3, sc2, sh2, g2, rwh, rwl, rb, exp_w1[1].astype(BF16), exp_w3[1].astype(BF16),
               exp_w2[1].astype(BF16), norm_final.astype(F32).reshape(1, d), n_lat, final=True)
    return out.reshape(b_, GRID_W, grows, d).transpose(0, 2, 1, 3).reshape(b_, n_lat, d)
```

```python
import functools
import math

import jax
import jax.numpy as jnp
from jax import lax
from jax.experimental import pallas as pl
from jax.experimental.pallas import tpu as pltpu

F32 = jnp.float32
BF16 = jnp.bfloat16
I32 = jnp.int32

D_MODEL = 1024
GRID_W = 64
EPS = 1e-6

S5_WIDTH = 512
S5_GROUP = 16
S5_GROUPS = 32
S5_STATE = 64
RG_WIDTH = 512
RG_HEADS = 8
RG_HEAD_DIM = 64
RG_C = 8.0
CONV_K = 4
HALF_GROUPS = S5_GROUPS // 2
HALF_STATE = HALF_GROUPS * S5_STATE
HALF_RG = RG_WIDTH // 2

M2_INNER = 2048
M2_HEAD_DIM = 64
M2_HEADS = 32
M2_GROUPS = 8
M2_HPG = 4
M2_STATE = 128
M2_CHUNK = 128
M2_CONV_DIM = M2_INNER + 2 * M2_GROUPS * M2_STATE

N_EXPERTS = 16
N_EXPERT_GROUPS = 4
EXPERTS_PER_GROUP = 4
N_CLASSES = 24
MOE_BLOCK = 256
ROW_W = D_MODEL + 128

TOK_TILE = 256
INPROJ1_COLS = 4
HALO = 8
SCAN_TILE = 128
VMEM_LIMIT = 52 * 1024 * 1024


def _cp(sem, vmem=VMEM_LIMIT):
    return pltpu.CompilerParams(dimension_semantics=sem, vmem_limit_bytes=vmem)


def _dot(a, b):
    return jnp.dot(a, b, preferred_element_type=F32)


def _split2(a):
    hi = a.astype(BF16)
    lo = (a - hi.astype(F32)).astype(BF16)
    return hi, lo


def _split3(a):
    a1 = a.astype(BF16)
    r1 = a - a1.astype(F32)
    a2 = r1.astype(BF16)
    a3 = (r1 - a2.astype(F32)).astype(BF16)
    return a1, a2, a3


def _dot3(a, b):
    ah, al = _split2(a)
    bh, bl = _split2(b)
    return _dot(ah, bh) + (_dot(al, bh) + _dot(ah, bl))


def _norm_mod(x, scale, shift):
    ms = jnp.mean(x * x, axis=-1, keepdims=True)
    return (x * lax.rsqrt(ms + EPS)) * scale + shift


def _silu(x):
    return x * jax.nn.sigmoid(x)


def _gelu(x):
    c = math.sqrt(2.0 / math.pi)
    return 0.5 * x * (1.0 + jnp.tanh(c * (x + 0.044715 * (x * x * x))))


def _softplus(x):
    return jnp.maximum(x, 0.0) + jnp.log(1.0 + jnp.exp(-jnp.abs(x)))


def _mod_kernel(c_ref, w_ref, b_ref, o_ref):
    c = c_ref[...]
    o_ref[...] = _dot3(_silu(c), w_ref[...]) + b_ref[...]


def _modulation(c8, w_mod, b_mod):
    depth, d, n = w_mod.shape
    tn = 1536
    return pl.pallas_call(
        _mod_kernel,
        grid=(depth, n // tn),
        in_specs=[pl.BlockSpec((8, d), lambda l, j: (0, 0)),
                  pl.BlockSpec((None, d, tn), lambda l, j: (l, 0, j)),
                  pl.BlockSpec((None, 1, tn), lambda l, j: (l, 0, j))],
        out_specs=pl.BlockSpec((None, 8, tn), lambda l, j: (l, 0, j)),
        out_shape=jax.ShapeDtypeStruct((depth, 8, n), F32),
        compiler_params=_cp(("parallel", "parallel")),
        name="modulation",
    )(c8, w_mod, b_mod.reshape(depth, 1, n))


def _inproj0_kernel(xp_ref, xc_ref, xn_ref, sc_ref, sh_ref, w_ref, cw_ref, cb_ref,
                    u_ref, ua_ref, xcf_ref, xca_ref, gr_ref, xr_s, *, nlt):
    i = pl.program_id(1)
    tt = TOK_TILE
    has_prev = jnp.logical_and(i > 0, i < nlt).astype(F32)
    has_next = (i < nlt - 1).astype(F32)
    xe = jnp.concatenate([xp_ref[...], xc_ref[...], xn_ref[...]], axis=0)
    h = _norm_mod(xe, sc_ref[...], sh_ref[...]).astype(BF16)
    p = _dot(h, w_ref[...])
    u = p[HALO:HALO + tt, 0:S5_WIDTH]
    gr = p[HALO:HALO + tt, S5_WIDTH + RG_WIDTH:]
    row = lax.broadcasted_iota(I32, (tt + 2 * HALO, 1), 0)
    keep = jnp.where(row < HALO, has_prev, jnp.where(row >= tt + HALO, has_next, 1.0))
    xr_s[...] = p[:, S5_WIDTH:S5_WIDTH + RG_WIDTH] * keep
    xc = cb_ref[...] + cw_ref[0:1, :] * xr_s[pl.ds(HALO - 1, tt), :]
    for k in range(1, CONV_K):
        xc = xc + cw_ref[k:k + 1, :] * xr_s[pl.ds(HALO - 1 + k, tt), :]
    lane = lax.broadcasted_iota(I32, (tt, S5_WIDTH), 1)
    first = lane < S5_WIDTH // 2
    u_ref[...] = u
    gr_ref[...] = gr
    xcf_ref[...] = xc
    ua_ref[...] = jnp.concatenate([jnp.where(first, u, 0.0), jnp.where(first, 0.0, u)], axis=1).astype(BF16)
    xca_ref[...] = jnp.concatenate([jnp.where(first, xc, 0.0), jnp.where(first, 0.0, xc)], axis=1).astype(BF16)


def _inproj0(xall, sc1, sh1, w_in, cw8, cb, n_lat):
    b_, nt, d = xall.shape
    tt = TOK_TILE
    nlt = n_lat // tt
    nti = nt // tt
    hb = tt // HALO
    nh = nt // HALO
    row_idx = lambda b, i: (jnp.where(i < nlt, b, b_), 0, 0)
    kern = functools.partial(_inproj0_kernel, nlt=nlt)
    return pl.pallas_call(
        kern,
        grid=(b_, nti),
        in_specs=[
            pl.BlockSpec((None, HALO, d), lambda b, i: (b, jnp.maximum(i * hb - 1, 0), 0)),
            pl.BlockSpec((None, tt, d), lambda b, i: (b, i, 0)),
            pl.BlockSpec((None, HALO, d), lambda b, i: (b, jnp.minimum(i * hb + hb, nh - 1), 0)),
            pl.BlockSpec((None, 1, d), row_idx),
            pl.BlockSpec((None, 1, d), row_idx),
            pl.BlockSpec(w_in.shape, lambda b, i: (0, 0)),
            pl.BlockSpec((8, RG_WIDTH), lambda b, i: (0, 0)),
            pl.BlockSpec((1, RG_WIDTH), lambda b, i: (0, 0)),
        ],
        out_specs=[
            pl.BlockSpec((None, tt, S5_WIDTH), lambda b, i: (b, i, 0)),
            pl.BlockSpec((tt, 2 * S5_WIDTH), lambda b, i: (i, b)),
            pl.BlockSpec((tt, RG_WIDTH), lambda b, i: (i, b)),
            pl.BlockSpec((tt, 2 * RG_WIDTH), lambda b, i: (i, b)),
            pl.BlockSpec((None, tt, RG_WIDTH), lambda b, i: (b, i, 0)),
        ],
        out_shape=[
            jax.ShapeDtypeStruct((b_, nt, S5_WIDTH), F32),
            jax.ShapeDtypeStruct((nt, b_ * 2 * S5_WIDTH), BF16),
            jax.ShapeDtypeStruct((nt, b_ * RG_WIDTH), F32),
            jax.ShapeDtypeStruct((nt, b_ * 2 * RG_WIDTH), BF16),
            jax.ShapeDtypeStruct((b_, nt, RG_WIDTH), F32),
        ],
        scratch_shapes=[pltpu.VMEM((tt + 2 * HALO, RG_WIDTH), F32)],
        compiler_params=_cp(("parallel", "parallel")),
        name="inproj0",
    )(xall, xall, xall, sc1, sh1, w_in, cw8, cb)


def _scan0_kernel(ua_ref, xca_ref, xcf_ref, bst_ref, cm_ref, ar_ref, ai_ref,
                  wa_ref, wx_ref, ba_ref, bx_ref, sp_ref,
                  ya_ref, hg_ref, s_s, ga_s, gb_s, hr_s, hi_s, hgs_s, *, rows):
    d = pl.program_id(0)
    j = pl.program_id(1)
    tts = SCAN_TILE
    r_tot = rows * tts

    @pl.when(j == 0)
    def _():
        hr_s[...] = jnp.zeros_like(hr_s)
        hi_s[...] = jnp.zeros_like(hi_s)
        hgs_s[...] = jnp.zeros_like(hgs_s)

    s_s[...] = _dot(ua_ref[...], bst_ref[...])

    xa = xca_ref[...]
    pa = _dot(xa, wa_ref[...]).reshape(tts, rows, HALF_RG) + ba_ref[...][None]
    px = _dot(xa, wx_ref[...]).reshape(tts, rows, HALF_RG) + bx_ref[...][None]
    log_a = (-RG_C) * jax.nn.sigmoid(pa) * sp_ref[...][None]
    beta = jnp.sqrt(1.0 - jnp.exp(2.0 * log_a))
    xc = xcf_ref[...].reshape(tts, rows, HALF_RG)
    ga_s[...] = jnp.exp(log_a).reshape(r_tot, HALF_RG)
    gb_s[...] = (beta * jax.nn.sigmoid(px) * xc).reshape(r_tot, HALF_RG)

    ar = ar_ref[...]
    ai = ai_ref[...]

    def step(k, carry):
        hr, hi, hg = carry
        t = jnp.where(d == 0, k, tts - 1 - k)
        r0 = pl.multiple_of(t * rows, rows)
        br = s_s[pl.ds(r0, rows), 0:HALF_STATE]
        bi = s_s[pl.ds(r0, rows), HALF_STATE:2 * HALF_STATE]
        nhr = ar * hr - ai * hi + br
        nhi = ar * hi + ai * hr + bi
        s_s[pl.ds(r0, rows), 0:HALF_STATE] = nhr
        s_s[pl.ds(r0, rows), HALF_STATE:2 * HALF_STATE] = nhi
        nhg = ga_s[pl.ds(r0, rows), :] * hg + gb_s[pl.ds(r0, rows), :]
        gb_s[pl.ds(r0, rows), :] = nhg
        return nhr, nhi, nhg

    hr, hi, hg = lax.fori_loop(0, tts, step, (hr_s[...], hi_s[...], hgs_s[...]))
    hr_s[...] = hr
    hi_s[...] = hi
    hgs_s[...] = hg
    ya_ref[...] = _dot(s_s[...].astype(BF16), cm_ref[...])
    hg_ref[...] = gb_s[...]


def _scan0(ua2, xca2, xcf2, bst, cm, ar, ai, wa, wx, ba, bx, sp, n_lat, rows):
    r_all = ua2.shape[0]
    tts = SCAN_TILE
    rt = rows * tts
    ntile = r_all // rt
    nlat = n_lat // tts
    nctx = ntile - nlat

    def tile(d, j):
        fwd = jnp.where(j < nctx, nlat + j, j - nctx)
        return jnp.where(d == 0, fwd, ntile - 1 - j)

    dsel = lambda d, j: (d, 0, 0)
    kern = functools.partial(_scan0_kernel, rows=rows)
    return pl.pallas_call(
        kern,
        grid=(2, ntile),
        in_specs=[
            pl.BlockSpec((rt, 2 * HALF_RG), lambda d, j: (tile(d, j), 0)),
            pl.BlockSpec((rt, 2 * HALF_RG), lambda d, j: (tile(d, j), 0)),
            pl.BlockSpec((rt, HALF_RG), lambda d, j: (tile(d, j), 0)),
            pl.BlockSpec((None,) + bst.shape[1:], dsel),
            pl.BlockSpec((None,) + cm.shape[1:], dsel),
            pl.BlockSpec((None,) + ar.shape[1:], dsel),
            pl.BlockSpec((None,) + ai.shape[1:], dsel),
            pl.BlockSpec((None,) + wa.shape[1:], dsel),
            pl.BlockSpec((None,) + wx.shape[1:], dsel),
            pl.BlockSpec((None,) + ba.shape[1:], dsel),
            pl.BlockSpec((None,) + bx.shape[1:], dsel),
            pl.BlockSpec((None,) + sp.shape[1:], dsel),
        ],
        out_specs=[
            pl.BlockSpec((None, rt, S5_WIDTH), lambda d, j: (d, tile(d, j), 0)),
            pl.BlockSpec((None, rt, HALF_RG), lambda d, j: (d, tile(d, j), 0)),
        ],
        out_shape=[
            jax.ShapeDtypeStruct((2, r_all, S5_WIDTH), F32),
            jax.ShapeDtypeStruct((2, r_all, HALF_RG), F32),
        ],
        scratch_shapes=[
            pltpu.VMEM((rt, 2 * HALF_STATE), F32),
            pltpu.VMEM((rt, HALF_RG), F32),
            pltpu.VMEM((rt, HALF_RG), F32),
            pltpu.VMEM((rows, HALF_STATE), F32),
            pltpu.VMEM((rows, HALF_STATE), F32),
            pltpu.VMEM((rows, HALF_RG), F32),
        ],
        compiler_params=_cp(("arbitrary", "arbitrary")),
        name="scan0",
    )(ua2, xca2, xcf2, bst, cm, ar, ai, wa, wx, ba, bx, sp)


def _merge0_kernel(ya_ref, hg_ref, u_ref, gr_ref, x_ref, g1_ref, d_ref, wglu_ref, wout_ref, o_ref):
    q = S5_WIDTH // 2
    ya = ya_ref[0] + ya_ref[1]
    y = jnp.concatenate([ya[:, 0:q], ya[:, 3 * q:4 * q]], axis=1) + d_ref[...] * u_ref[...]
    y = _gelu(y)
    y = y * jax.nn.sigmoid(_dot(y.astype(BF16), wglu_ref[...]))
    g = (hg_ref[0] + hg_ref[1]) * _gelu(gr_ref[...])
    m = _dot(jnp.concatenate([y, g], axis=1).astype(BF16), wout_ref[...])
    o_ref[...] = x_ref[...] + g1_ref[...] * m


def _merge0(ya, hg, u, gr, xall, g1, s5d, wglu, wout, n_lat):
    b_, nt, d = xall.shape
    tt = TOK_TILE
    nlt = n_lat // tt
    row_idx = lambda b, i: (jnp.where(i < nlt, b, b_), 0, 0)
    return pl.pallas_call(
        _merge0_kernel,
        grid=(b_, nt // tt),
        in_specs=[
            pl.BlockSpec((2, tt, 2 * S5_WIDTH), lambda b, i: (0, i, b)),
            pl.BlockSpec((2, tt, RG_WIDTH), lambda b, i: (0, i, b)),
            pl.BlockSpec((None, tt, S5_WIDTH), lambda b, i: (b, i, 0)),
            pl.BlockSpec((None, tt, RG_WIDTH), lambda b, i: (b, i, 0)),
            pl.BlockSpec((None, tt, d), lambda b, i: (b, i, 0)),
            pl.BlockSpec((None, 1, d), row_idx),
            pl.BlockSpec((1, S5_WIDTH), lambda b, i: (0, 0)),
            pl.BlockSpec(wglu.shape, lambda b, i: (0, 0)),
            pl.BlockSpec(wout.shape, lambda b, i: (0, 0)),
        ],
        out_specs=pl.BlockSpec((None, tt, d), lambda b, i: (b, i, 0)),
        out_shape=jax.ShapeDtypeStruct((b_, nt, d), F32),
        compiler_params=_cp(("parallel", "parallel")),
        name="merge0",
    )(ya, hg, u, gr, xall, g1, s5d, wglu, wout)


def _route_kernel(x_ref, sc_ref, sh_ref, rwh_ref, rwl_ref, rb_ref, hb_ref, meta_ref, cnt_ref, carry_s):
    first = jnp.logical_and(pl.program_id(0) == 0, pl.program_id(1) == 0)

    @pl.when(first)
    def _():
        carry_s[...] = jnp.zeros_like(carry_s)

    tt = TOK_TILE
    h = _norm_mod(x_ref[...], sc_ref[...], sh_ref[...])
    hh, hl = _split2(h)
    logits = _dot(hh, rwh_ref[...]) + (_dot(hl, rwh_ref[...]) + _dot(hh, rwl_ref[...]))
    lane = lax.broadcasted_iota(I32, (tt, 128), 1)
    valid = lane < N_EXPERTS
    neg = -1e30
    lg = jnp.where(valid, logits, neg)
    mx = jnp.max(lg, axis=-1, keepdims=True)
    ex = jnp.where(valid, jnp.exp(lg - mx), 0.0)
    probs = ex / jnp.sum(ex, axis=-1, keepdims=True)
    sel = probs + rb_ref[...]
    grp = jnp.right_shift(lane, 2)
    best = jnp.zeros((tt, 1), I32)
    bestv = jnp.max(jnp.where(jnp.logical_and(valid, grp == 0), sel, neg), axis=-1, keepdims=True)
    for k in range(1, N_EXPERT_GROUPS):
        gk = jnp.max(jnp.where(jnp.logical_and(valid, grp == k), sel, neg), axis=-1, keepdims=True)
        upd = gk > bestv
        best = jnp.where(upd, k, best)
        bestv = jnp.where(upd, gk, bestv)
    msel = jnp.where(jnp.logical_and(valid, grp == best), sel, neg)
    v1 = jnp.max(msel, axis=-1, keepdims=True)
    i1 = jnp.min(jnp.where(msel == v1, lane, 128), axis=-1, keepdims=True)
    msel2 = jnp.where(lane == i1, neg, msel)
    v2 = jnp.max(msel2, axis=-1, keepdims=True)
    i2 = jnp.min(jnp.where(msel2 == v2, lane, 128), axis=-1, keepdims=True)
    p1 = jnp.sum(jnp.where(lane == i1, probs, 0.0), axis=-1, keepdims=True)
    p2 = jnp.sum(jnp.where(lane == i2, probs, 0.0), axis=-1, keepdims=True)
    den = p1 + p2
    g1 = p1 / den
    g2 = p2 / den
    lo = jnp.minimum(i1, i2) - EXPERTS_PER_GROUP * best
    hi = jnp.maximum(i1, i2) - EXPERTS_PER_GROUP * best
    g_lo = jnp.where(i1 < i2, g1, g2)
    g_hi = jnp.where(i1 < i2, g2, g1)
    pair = jnp.right_shift(lo * (7 - lo), 1) + (hi - lo - 1)
    cls = best * 6 + pair

    onehot = jnp.where(lane == cls, 1.0, 0.0)
    r_i = lax.broadcasted_iota(I32, (tt, tt), 0)
    c_i = lax.broadcasted_iota(I32, (tt, tt), 1)
    before = jnp.where(c_i < r_i, 1.0, 0.0).astype(BF16)
    cum = _dot(before, onehot.astype(BF16))
    rank = jnp.sum(onehot * (cum + carry_s[...]), axis=-1, keepdims=True)
    carry_s[...] = carry_s[...] + jnp.sum(onehot, axis=0, keepdims=True)

    hb_ref[:, 0:D_MODEL] = h
    hb_ref[:, D_MODEL:] = jnp.where(lane == 0, g_lo, jnp.where(lane == 1, g_hi, 0.0))
    meta = jnp.where(lane == 0, cls.astype(F32), jnp.where(lane == 1, rank, 0.0))
    meta_ref[...] = meta.T[0:8, :].astype(I32)
    cnt_ref[...] = jnp.broadcast_to(carry_s[...], (8, 128))


def _route(x, sc2, sh2, rwh, rwl, rb, n_lat):
    b_, nt, d = x.shape
    tt = TOK_TILE
    nlt = n_lat // tt
    nti = nt // tt
    row_idx = lambda b, i: (jnp.where(i < nlt, b, b_), 0, 0)
    flat = lambda b, i: (b * nti + i, 0)
    return pl.pallas_call(
        _route_kernel,
        grid=(b_, nti),
        in_specs=[
            pl.BlockSpec((None, tt, d), lambda b, i: (b, i, 0)),
            pl.BlockSpec((None, 1, d), row_idx),
            pl.BlockSpec((None, 1, d), row_idx),
            pl.BlockSpec((d, 128), lambda b, i: (0, 0)),
            pl.BlockSpec((d, 128), lambda b, i: (0, 0)),
            pl.BlockSpec((1, 128), lambda b, i: (0, 0)),
        ],
        out_specs=[
            pl.BlockSpec((tt, ROW_W), flat),
            pl.BlockSpec((None, 8, tt), lambda b, i: (b * nti + i, 0, 0)),
            pl.BlockSpec((8, 128), lambda b, i: (0, 0)),
        ],
        out_shape=[
            jax.ShapeDtypeStruct((b_ * nt, ROW_W), F32),
            jax.ShapeDtypeStruct((b_ * nti, 8, tt), I32),
            jax.ShapeDtypeStruct((8, 128), F32),
        ],
        scratch_shapes=[pltpu.VMEM((1, 128), F32)],
        compiler_params=_cp(("arbitrary", "arbitrary")),
        name="route",
    )(x, sc2, sh2, rwh, rwl, rb)


def _row_copy(src, dst, src_row, dst_row, sem):
    return pltpu.make_async_copy(src.at[pl.ds(src_row, 1)], dst.at[pl.ds(dst_row, 1)], sem)


def _dest_row(ps_ref, meta_ref, r):
    return ps_ref[meta_ref[0, r]] + meta_ref[1, r]


ISSUE_UNROLL = 8


def _dispatch_kernel(ps_ref, meta_ref, hb_ref, xb_in, xb_out, sems):
    del xb_in
    tt = TOK_TILE
    i = pl.program_id(0)
    n = pl.num_programs(0)
    slot = lax.rem(i, 2)
    base = i * tt

    def issue(r, carry):
        _row_copy(hb_ref, xb_out, base + r, _dest_row(ps_ref, meta_ref, r), sems.at[slot]).start()
        return carry

    lax.fori_loop(0, tt, issue, 0, unroll=ISSUE_UNROLL)

    def drain(s):
        pltpu.make_async_copy(hb_ref.at[pl.ds(0, tt)], xb_out.at[pl.ds(0, tt)], sems.at[s]).wait()

    @pl.when(i > 0)
    def _():
        drain(1 - slot)

    @pl.when(i == n - 1)
    def _():
        drain(slot)


def _dispatch(pstart, meta, hb, n_rows):
    t = hb.shape[0]
    tt = TOK_TILE
    zeros = jnp.zeros((n_rows, ROW_W), F32)
    gs = pltpu.PrefetchScalarGridSpec(
        num_scalar_prefetch=1,
        grid=(t // tt,),
        in_specs=[
            pl.BlockSpec((None, 8, tt), lambda i, ps: (i, 0, 0), memory_space=pltpu.SMEM),
            pl.BlockSpec(memory_space=pl.ANY),
            pl.BlockSpec(memory_space=pl.ANY),
        ],
        out_specs=pl.BlockSpec(memory_space=pl.ANY),
        scratch_shapes=[pltpu.SemaphoreType.DMA((2,))],
    )
    return pl.pallas_call(
        _dispatch_kernel,
        grid_spec=gs,
        out_shape=jax.ShapeDtypeStruct((n_rows, ROW_W), F32),
        input_output_aliases={3: 0},
        compiler_params=_cp(("arbitrary",)),
        name="dispatch",
    )(pstart, meta, hb, zeros)


def _expert_kernel(ea_ref, eb_ref, nu_ref, x_ref, w1a, w1b, w3a, w3b, w2a, w2b, o_ref):
    del ea_ref, eb_ref
    j = pl.program_id(0)

    @pl.when(j < nu_ref[0])
    def _():
        x = x_ref[:, 0:D_MODEL].astype(BF16)
        g_lo = x_ref[:, D_MODEL:D_MODEL + 1]
        g_hi = x_ref[:, D_MODEL + 1:D_MODEL + 2]

        def ffn(w1, w3, w2):
            a = _dot(x, w1[...])
            b = _dot(x, w3[...])
            return _dot((_silu(a) * b).astype(BF16), w2[...])

        o_ref[...] = g_lo * ffn(w1a, w3a, w2a) + g_hi * ffn(w1b, w3b, w2b)

    @pl.when(j >= nu_ref[0])
    def _():
        o_ref[...] = jnp.zeros_like(o_ref)


def _experts(blk_ea, blk_eb, nused, xb, w1, w3, w2, layer):
    n_rows = xb.shape[0]
    nb = n_rows // MOE_BLOCK
    d = D_MODEL
    wa = lambda j, ea, eb, nu: (layer, ea[j], 0, 0)
    wb = lambda j, ea, eb, nu: (layer, eb[j], 0, 0)
    wspec = lambda f: pl.BlockSpec((None, None, d, d), f)
    gs = pltpu.PrefetchScalarGridSpec(
        num_scalar_prefetch=3,
        grid=(nb,),
        in_specs=[pl.BlockSpec((MOE_BLOCK, ROW_W), lambda j, ea, eb, nu: (j, 0)),
                  wspec(wa), wspec(wb), wspec(wa), wspec(wb), wspec(wa), wspec(wb)],
        out_specs=pl.BlockSpec((MOE_BLOCK, d), lambda j, ea, eb, nu: (j, 0)),
    )
    return pl.pallas_call(
        _expert_kernel,
        grid_spec=gs,
        out_shape=jax.ShapeDtypeStruct((n_rows, d), F32),
        compiler_params=_cp(("arbitrary",)),
        name="experts",
    )(blk_ea, blk_eb, nused, xb, w1, w1, w3, w3, w2, w2)


def _combine_kernel(ps_ref, meta_ref, metan_ref, yb_ref, x_ref, g2_ref, nw_ref, o_ref, buf, sems, *, final):
    tt = TOK_TILE
    i = pl.program_id(0)
    n = pl.num_programs(0)
    slot = lax.rem(i, 2)

    def gather(mref, s):
        def issue(r, carry):
            _row_copy(yb_ref, buf.at[s], _dest_row(ps_ref, mref, r), r, sems.at[s]).start()
            return carry

        lax.fori_loop(0, tt, issue, 0, unroll=ISSUE_UNROLL)

    @pl.when(i == 0)
    def _():
        gather(meta_ref, 0)

    @pl.when(i + 1 < n)
    def _():
        gather(metan_ref, 1 - slot)

    pltpu.make_async_copy(yb_ref.at[pl.ds(0, tt)], buf.at[slot], sems.at[slot]).wait()
    x2 = x_ref[...] + g2_ref[...] * buf[slot]
    if final:
        ms = jnp.mean(x2 * x2, axis=-1, keepdims=True)
        x2 = x2 * lax.rsqrt(ms + EPS) * nw_ref[...]
    o_ref[...] = x2


def _combine(pstart, meta, yb, x, g2, nw, n_lat, final):
    b_, nt, d = x.shape
    tt = TOK_TILE
    nlt = n_lat // tt
    nti = nt // tt
    ntile = b_ * nti
    row_idx = lambda i, ps: (jnp.where(lax.rem(i, nti) < nlt, i // nti, b_), 0, 0)
    kern = functools.partial(_combine_kernel, final=final)
    gs = pltpu.PrefetchScalarGridSpec(
        num_scalar_prefetch=1,
        grid=(ntile,),
        in_specs=[
            pl.BlockSpec((None, 8, tt), lambda i, ps: (i, 0, 0), memory_space=pltpu.SMEM),
            pl.BlockSpec((None, 8, tt), lambda i, ps: (jnp.minimum(i + 1, ntile - 1), 0, 0),
                         memory_space=pltpu.SMEM),
            pl.BlockSpec(memory_space=pl.ANY),
            pl.BlockSpec((tt, d), lambda i, ps: (i, 0)),
            pl.BlockSpec((None, 1, d), row_idx),
            pl.BlockSpec((1, d), lambda i, ps: (0, 0)),
        ],
        out_specs=pl.BlockSpec((tt, d), lambda i, ps: (i, 0)),
        scratch_shapes=[pltpu.VMEM((2, tt, d), F32), pltpu.SemaphoreType.DMA((2,))],
    )
    out = pl.pallas_call(
        kern,
        grid_spec=gs,
        out_shape=jax.ShapeDtypeStruct((b_ * nt, d), F32),
        compiler_params=_cp(("arbitrary",)),
        name="combine",
    )(pstart, meta, meta, yb, x.reshape(b_ * nt, d), g2, nw)
    return out.reshape(b_, nt, d)


def _class_tables():
    ea, eb = [], []
    for g in range(N_EXPERT_GROUPS):
        for lo in range(EXPERTS_PER_GROUP):
            for hi in range(lo + 1, EXPERTS_PER_GROUP):
                ea.append(g * EXPERTS_PER_GROUP + lo)
                eb.append(g * EXPERTS_PER_GROUP + hi)
    return jnp.array(ea, I32), jnp.array(eb, I32)


def _moe(x, sc2, sh2, g2, rwh, rwl, rb, w1, w3, w2, layer, nw, n_lat, final):
    b_, nt, d = x.shape
    t = b_ * nt
    hb, meta, cnt = _route(x, sc2, sh2, rwh, rwl, rb, n_lat)
    counts = cnt[0, :N_CLASSES].astype(I32)
    padded = (counts + MOE_BLOCK - 1) // MOE_BLOCK * MOE_BLOCK
    pend = jnp.cumsum(padded)
    pstart = pend - padded
    nb = t // MOE_BLOCK + N_CLASSES
    blk_start = jnp.arange(nb, dtype=I32)[:, None] * MOE_BLOCK
    blk_cls = jnp.minimum(jnp.sum((pend[None, :] <= blk_start).astype(I32), axis=1), N_CLASSES - 1)
    ea, eb = _class_tables()
    nused = (pend[-1] // MOE_BLOCK).reshape(1).astype(I32)
    xb = _dispatch(pstart, meta, hb, nb * MOE_BLOCK)
    yb = _experts(jnp.take(ea, blk_cls), jnp.take(eb, blk_cls), nused, xb, w1, w3, w2, layer)
    return _combine(pstart, meta, yb, x, g2, nw, n_lat, final)


def _inproj1_kernel(*refs, latent, nq):
    if latent:
        (xp_ref, xm_ref, xn_ref, sc_ref, sh_ref, wz_ref, wx_ref, wd_ref, cw_ref, cb_ref, db_ref,
         z_ref, xbc_ref, dt_ref, scr) = refs
        q = pl.program_id(1)
        has_prev = (q > 0).astype(F32)
        has_next = (q < nq - 1).astype(F32)
        cols = [xm_ref[k] for k in range(xm_ref.shape[0])]
        xe = jnp.concatenate([xp_ref[...]] + cols + [xn_ref[...]], axis=0)
    else:
        (xm_ref, sc_ref, sh_ref, wz_ref, wx_ref, wd_ref, cw_ref, cb_ref, db_ref,
         z_ref, xbc_ref, dt_ref, scr) = refs
        has_prev = 0.0
        has_next = 0.0
        pad = jnp.zeros((HALO, D_MODEL), F32)
        xe = jnp.concatenate([pad, xm_ref[...], pad], axis=0)
    tt = z_ref.shape[0]
    h = _norm_mod(xe, sc_ref[...], sh_ref[...]).astype(BF16)
    hc = h[HALO:HALO + tt]
    z_ref[...] = _dot(hc, wz_ref[...]).astype(BF16)
    row = lax.broadcasted_iota(I32, (tt + 2 * HALO, 1), 0)
    keep = jnp.where(row < HALO, has_prev, jnp.where(row >= tt + HALO, has_next, 1.0))
    pw = 1024
    for pc in range(M2_CONV_DIM // pw):
        sl = slice(pc * pw, (pc + 1) * pw)
        scr[...] = _dot(h, wx_ref[:, sl]) * keep
        acc = cb_ref[:, sl] + cw_ref[0:1, sl] * scr[pl.ds(HALO - 1, tt), :]
        for k in range(1, CONV_K):
            acc = acc + cw_ref[k:k + 1, sl] * scr[pl.ds(HALO - 1 + k, tt), :]
        xbc_ref[:, sl] = _silu(acc).astype(BF16)
    lane = lax.broadcasted_iota(I32, (tt, 128), 1)
    dt = _softplus(_dot(hc, wd_ref[...]) + db_ref[...])
    dt_ref[...] = jnp.where(lane < 2 * M2_HEADS, dt, 0.0)


def _inproj1(xall, xcol, sc1, sh1, wz, wx, wd, cw8, cb, db, n_lat, latent):
    b_, nt, d = xall.shape
    if latent:
        n = n_lat
        rows = n_lat // GRID_W
        nc = INPROJ1_COLS
        tt = nc * rows
        nq = GRID_W // nc
        rb = rows // HALO
        x_specs = [
            pl.BlockSpec((None, None, HALO, d), lambda b, q: (b, jnp.maximum(nc * q - 1, 0), rb - 1, 0)),
            pl.BlockSpec((None, nc, rows, d), lambda b, q: (b, q, 0, 0)),
            pl.BlockSpec((None, None, HALO, d), lambda b, q: (b, jnp.minimum(nc * q + nc, GRID_W - 1), 0, 0)),
        ]
        x_args = [xcol, xcol, xcol]
        row_idx = lambda b, q: (b, 0, 0)
    else:
        tt = TOK_TILE
        n = nt - n_lat
        nq = n // tt
        off = n_lat // tt
        x_specs = [pl.BlockSpec((None, tt, d), lambda b, q: (b, off + q, 0))]
        x_args = [xall]
        row_idx = lambda b, q: (b_, 0, 0)
    const = lambda b, q: (0, 0)
    kern = functools.partial(_inproj1_kernel, latent=latent, nq=nq)
    return pl.pallas_call(
        kern,
        grid=(b_, nq),
        in_specs=x_specs + [
            pl.BlockSpec((None, 1, d), row_idx),
            pl.BlockSpec((None, 1, d), row_idx),
            pl.BlockSpec(wz.shape, const, pipeline_mode=pl.Buffered(1)),
            pl.BlockSpec(wx.shape, const, pipeline_mode=pl.Buffered(1)),
            pl.BlockSpec(wd.shape, const, pipeline_mode=pl.Buffered(1)),
            pl.BlockSpec(cw8.shape, const),
            pl.BlockSpec(cb.shape, const),
            pl.BlockSpec(db.shape, const),
        ],
        out_specs=[
            pl.BlockSpec((None, tt, M2_INNER), lambda b, q: (b, q, 0)),
            pl.BlockSpec((None, tt, M2_CONV_DIM), lambda b, q: (b, q, 0)),
            pl.BlockSpec((None, tt, 128), lambda b, q: (b, q, 0)),
        ],
        out_shape=[
            jax.ShapeDtypeStruct((b_, n, M2_INNER), BF16),
            jax.ShapeDtypeStruct((b_, n, M2_CONV_DIM), BF16),
            jax.ShapeDtypeStruct((b_, n, 128), F32),
        ],
        scratch_shapes=[pltpu.VMEM((tt + 2 * HALO, 1024), F32)],
        compiler_params=_cp(("parallel", "parallel")),
        name="inproj1_lat" if latent else "inproj1_ctx",
    )(*x_args, sc1, sh1, wz, wx, wd, cw8, cb, db)


def _ssd_kernel(*refs, reverse, need_y, lane0):
    if need_y:
        xbc_ref, dt_ref, arow_ref, e_ref, s0_ref, y_ref, sf_ref, s_s = refs
    else:
        xbc_ref, dt_ref, arow_ref, e_ref, s0_ref, sf_ref, s_s = refs
    c = pl.program_id(1)
    nc = pl.num_programs(1)
    q = M2_CHUNK

    @pl.when(c == 0)
    def _():
        s_s[...] = s0_ref[...]

    dt = dt_ref[...]
    a = dt * arow_ref[...]
    row = lax.broadcasted_iota(I32, (q, q), 0)
    col = lax.broadcasted_iota(I32, (q, q), 1)
    incl = (col >= row) if reverse else (col <= row)
    incl_t = (row >= col) if reverse else (row <= col)
    lt = jnp.where(incl, 1.0, 0.0).astype(BF16)
    lt_t = jnp.where(incl_t, 1.0, 0.0).astype(BF16)
    a1, a2, a3 = _split3(a)
    cum = _dot(lt, a1) + (_dot(lt, a2) + _dot(lt, a3))
    b1, b2, b3 = _split3(a.T)
    cum_t = _dot(b1, lt_t) + (_dot(b2, lt_t) + _dot(b3, lt_t))
    atot = cum[0:1, :] if reverse else cum[q - 1:q, :]
    wexp = _dot((dt * jnp.exp(atot - cum)).astype(BF16), e_ref[...])
    dt_t = dt.T
    eh, el = _split2(jnp.broadcast_to(jnp.exp(atot), (8, 128)))
    atx = _dot(eh, e_ref[...]) + _dot(el, e_ref[...])

    for g in range(M2_GROUPS):
        bg = xbc_ref[:, M2_INNER + M2_STATE * g:M2_INNER + M2_STATE * (g + 1)]
        gw = M2_HPG * M2_HEAD_DIM
        xg = xbc_ref[:, gw * g:gw * (g + 1)]
        sg = s_s[g]
        if need_y:
            cg = xbc_ref[:, M2_INNER + M2_GROUPS * M2_STATE + M2_STATE * g:
                         M2_INNER + M2_GROUPS * M2_STATE + M2_STATE * (g + 1)]
            cb = lax.dot_general(cg, bg, (((1,), (1,)), ((), ())), preferred_element_type=F32)
            cg32 = cg.astype(F32)
            sgb = sg.astype(BF16)
            y_parts = []
            for r in range(M2_HPG):
                hl = lane0 + M2_HPG * g + r
                ps = slice(M2_HEAD_DIM * r, M2_HEAD_DIM * (r + 1))
                colb = jnp.broadcast_to(cum[:, hl:hl + 1], (q, q))
                rowb = jnp.broadcast_to(cum_t[hl:hl + 1, :], (q, q))
                lmat = jnp.exp(jnp.where(incl, colb - rowb, -1e30))
                gm = (cb * lmat * dt_t[hl:hl + 1, :]).astype(BF16)
                cd = (jnp.exp(colb) * cg32).astype(BF16)
                lhs = jnp.concatenate([gm, cd], axis=1)
                rhs = jnp.concatenate([xg[:, ps], sgb[:, ps]], axis=0)
                y_parts.append(_dot(lhs, rhs))
        xw = (xg.astype(F32) * wexp[:, gw * g:gw * (g + 1)]).astype(BF16)
        snew = lax.dot_general(bg, xw, (((0,), (0,)), ((), ())), preferred_element_type=F32)
        s_s[g] = atx[0:1, gw * g:gw * (g + 1)] * sg + snew
        if need_y:
            y_ref[:, gw * g:gw * (g + 1)] = jnp.concatenate(y_parts, axis=1)

    @pl.when(c == nc - 1)
    def _():
        sf_ref[...] = s_s[...]


def _ssd(xbc, dt, arow, emat, s0, reverse, need_y):
    b_, n, _ = xbc.shape
    q = M2_CHUNK
    nc = n // q
    cidx = (lambda c: nc - 1 - c) if reverse else (lambda c: c)
    sshape = (M2_GROUPS, M2_STATE, M2_HPG * M2_HEAD_DIM)
    kern = functools.partial(_ssd_kernel, reverse=reverse, need_y=need_y, lane0=M2_HEADS if reverse else 0)
    out_specs = [pl.BlockSpec((None,) + sshape, lambda b, c: (b, 0, 0, 0))]
    out_shape = [jax.ShapeDtypeStruct((b_,) + sshape, F32)]
    if need_y:
        out_specs = [pl.BlockSpec((None, q, M2_INNER), lambda b, c: (b, cidx(c), 0))] + out_specs
        out_shape = [jax.ShapeDtypeStruct((b_, n, M2_INNER), F32)] + out_shape
    return pl.pallas_call(
        kern,
        grid=(b_, nc),
        in_specs=[
            pl.BlockSpec((None, q, M2_CONV_DIM), lambda b, c: (b, cidx(c), 0)),
            pl.BlockSpec((None, q, 128), lambda b, c: (b, cidx(c), 0)),
            pl.BlockSpec((1, 128), lambda b, c: (0, 0)),
            pl.BlockSpec(emat.shape, lambda b, c: (0, 0)),
            pl.BlockSpec((None,) + sshape, lambda b, c: (b, 0, 0, 0)),
        ],
        out_specs=out_specs,
        out_shape=out_shape,
        scratch_shapes=[pltpu.VMEM(sshape, F32)],
        compiler_params=_cp(("arbitrary", "arbitrary")),
        name=("ssd_lat" if need_y else "ssd_ctx") + ("_bwd" if reverse else "_fwd"),
    )(xbc, dt, arow, emat, s0)


def _fin1_kernel(yf_ref, yb_ref, z_ref, xs_ref, x_ref, g1_ref, dx_ref, nw_ref, wout_ref, o_ref):
    y = yf_ref[...] + yb_ref[...] + dx_ref[...] * xs_ref[...].astype(F32)
    y = y * _silu(z_ref[...].astype(F32))
    ms = jnp.mean(y * y, axis=-1, keepdims=True)
    yn = (y * lax.rsqrt(ms + EPS)) * nw_ref[...]
    m = _dot(yn.astype(BF16), wout_ref[...])
    x = jnp.concatenate([x_ref[k] for k in range(x_ref.shape[0])], axis=0)
    o_ref[...] = x + g1_ref[...] * m


def _fin1(yf, yb, z, xbc, xcol, g1, dx, nw, wout, n_lat):
    b_, _, _, d = xcol.shape
    n = n_lat
    rows = n_lat // GRID_W
    tt = TOK_TILE
    const = lambda b, q: (0, 0)
    tok = lambda w: pl.BlockSpec((None, tt, w), lambda b, q: (b, q, 0))
    return pl.pallas_call(
        _fin1_kernel,
        grid=(b_, n // tt),
        in_specs=[
            tok(M2_INNER), tok(M2_INNER), tok(M2_INNER), tok(M2_INNER),
            pl.BlockSpec((None, tt // rows, rows, d), lambda b, q: (b, q, 0, 0)),
            pl.BlockSpec((None, 1, d), lambda b, q: (b, 0, 0)),
            pl.BlockSpec((1, M2_INNER), const),
            pl.BlockSpec((1, M2_INNER), const),
            pl.BlockSpec(wout.shape, const),
        ],
        out_specs=tok(d),
        out_shape=jax.ShapeDtypeStruct((b_, n, d), F32),
        compiler_params=_cp(("parallel", "parallel")),
        name="fin1",
    )(yf, yb, z, xbc, xcol, g1, dx, nw, wout)


def _s5_params(lam_re, lam_im, log_dt, b_re, b_im, c_re, c_im, rows):
    lam_re = lam_re.astype(F32)
    lam_im = lam_im.astype(F32)
    dt = jnp.exp(log_dt.astype(F32))[..., None]
    mag = jnp.exp(lam_re * dt)
    ar = mag * jnp.cos(lam_im * dt)
    ai = mag * jnp.sin(lam_im * dt)
    den = lam_re * lam_re + lam_im * lam_im
    fr = ((ar - 1.0) * lam_re + ai * lam_im) / den
    fi = (ai * lam_re - (ar - 1.0) * lam_im) / den
    bbr = fr[..., None] * b_re - fi[..., None] * b_im
    bbi = fr[..., None] * b_im + fi[..., None] * b_re
    hg = HALF_GROUPS
    eye = jnp.eye(hg, dtype=F32)

    def in_mat(bb):
        bb = bb.reshape(2, 2, hg, S5_STATE, S5_GROUP)
        m = jnp.einsum('dhgpk,gj->dhgkjp', bb, eye)
        return m.reshape(2, 2 * hg * S5_GROUP, hg * S5_STATE)

    bst = jnp.concatenate([in_mat(bbr), in_mat(bbi)], axis=-1).astype(BF16)

    def out_mat(cc):
        cc = cc.astype(F32).reshape(2, 2, hg, S5_GROUP, S5_STATE)
        m = jnp.einsum('dhgkp,gj->dgphjk', cc, eye)
        return m.reshape(2, hg * S5_STATE, 2 * hg * S5_GROUP)

    cm = jnp.concatenate([out_mat(c_re), -out_mat(c_im)], axis=1).astype(BF16)
    half_rows = lambda v: jnp.tile(v.reshape(2, 2, HALF_STATE), (1, rows // 2, 1))
    return bst, cm, half_rows(ar), half_rows(ai)


def _rg_params(wa, ba, wx, bx, lam, rows):
    hh = RG_HEADS // 2
    eye = jnp.eye(hh, dtype=F32)

    def stack(w):
        w = w.astype(F32).reshape(2, 2, hh, RG_HEAD_DIM, RG_HEAD_DIM)
        m = jnp.einsum('dhgij,gk->dhgikj', w, eye)
        return m.reshape(2, RG_WIDTH, HALF_RG).astype(BF16)

    half_rows = lambda v: jnp.tile(v.astype(F32).reshape(2, 2, HALF_RG), (1, rows // 2, 1))
    sp = jax.nn.softplus(-lam.astype(F32))
    return stack(wa), stack(wx), half_rows(ba), half_rows(bx), half_rows(sp)


def kernel(x, c, ctx, c_ctx, w_mod, b_mod, norm_mix, norm_ffn, norm_final, router_w, router_b, exp_w1, exp_w3, exp_w2, ab_w_in, ab_w_out, s5_lam_re, s5_lam_im, s5_log_dt, s5_b_re, s5_b_im, s5_c_re, s5_c_im, s5_d, s5_w_glu, rg_conv_w, rg_conv_b, rg_wa, rg_ba, rg_wx, rg_bx, rg_lam, m2_w_in, m2_conv_w, m2_conv_b, m2_dt_bias, m2_a_log, m2_d, m2_norm, m2_w_out):
    b_, n_lat, d = x.shape
    n_ctx = ctx.shape[1]
    rows = 2 * b_
    assert d == D_MODEL and rows % 8 == 0
    assert n_lat % (GRID_W * HALO) == 0 and n_lat // GRID_W == M2_CHUNK
    assert n_lat % TOK_TILE == 0 and n_ctx % TOK_TILE == 0 and n_ctx % GRID_W == 0
    nt = n_lat + n_ctx

    xall = jnp.concatenate([x, ctx], axis=1)
    c8 = jnp.concatenate([c, c_ctx[None], jnp.zeros((7 - b_, d), F32)], axis=0)
    mods = _modulation(c8, w_mod, b_mod)
    nrow = b_ + 1

    def mod_rows(layer):
        m = mods[layer, :nrow].reshape(nrow, 6, 1, d)
        sh1, sc1, g1, sh2, sc2, g2 = (m[:, k] for k in range(6))
        return (norm_mix[layer] * (1.0 + sc1), sh1, g1, norm_ffn[layer] * (1.0 + sc2), sh2, g2)

    rwh, rwl = _split2(jnp.pad(router_w.astype(F32), ((0, 0), (0, 128 - N_EXPERTS))))
    rb = jnp.pad(router_b.astype(F32), (0, 128 - N_EXPERTS)).reshape(1, 128)
    one_row = jnp.ones((1, d), F32)

    sc1, sh1, g1, sc2, sh2, g2 = mod_rows(0)
    cw8 = jnp.pad(rg_conv_w[0].astype(F32), ((0, 8 - CONV_K), (0, 0)))
    u, ua, xcf, xca, gr = _inproj0(xall, sc1, sh1, ab_w_in[0].astype(BF16), cw8,
                                   rg_conv_b[0].astype(F32).reshape(1, RG_WIDTH), n_lat)
    bst, cm, ar, ai = _s5_params(s5_lam_re[0], s5_lam_im[0], s5_log_dt[0], s5_b_re[0], s5_b_im[0],
                                 s5_c_re[0], s5_c_im[0], rows)
    wa, wx, ba, bx, sp = _rg_params(rg_wa[0], rg_ba[0], rg_wx[0], rg_bx[0], rg_lam[0], rows)
    ya, hg = _scan0(ua.reshape(nt * rows, S5_WIDTH), xca.reshape(nt * rows, RG_WIDTH),
                    xcf.reshape(nt * rows, HALF_RG), bst, cm, ar, ai, wa, wx, ba, bx, sp, n_lat, rows)
    x1 = _merge0(ya.reshape(2, nt, b_ * 2 * S5_WIDTH), hg.reshape(2, nt, b_ * RG_WIDTH), u, gr, xall, g1,
                 s5_d[0].astype(F32).reshape(1, S5_WIDTH), s5_w_glu[0].astype(BF16),
                 ab_w_out[0].astype(BF16), n_lat)
    ew1, ew3, ew2 = exp_w1.astype(BF16), exp_w3.astype(BF16), exp_w2.astype(BF16)
    x2 = _moe(x1, sc2, sh2, g2, rwh, rwl, rb, ew1, ew3, ew2, 0, one_row, n_lat, final=False)

    sc1, sh1, g1, sc2, sh2, g2 = mod_rows(1)
    w_in = m2_w_in[0]
    wz = w_in[:, :M2_INNER].astype(BF16)
    wxbc = w_in[:, M2_INNER:M2_INNER + M2_CONV_DIM].astype(BF16)
    wd = jnp.pad(w_in[:, M2_INNER + M2_CONV_DIM:], ((0, 0), (0, 128 - 2 * M2_HEADS))).astype(BF16)
    cw8 = jnp.pad(m2_conv_w[0].astype(F32), ((0, 8 - CONV_K), (0, 0)))
    cb = m2_conv_b[0].astype(F32).reshape(1, M2_CONV_DIM)
    db = jnp.pad(m2_dt_bias[0].astype(F32).reshape(2 * M2_HEADS), (0, 128 - 2 * M2_HEADS)).reshape(1, 128)
    grows = n_lat // GRID_W
    x2v = x2.reshape(b_, nt // GRID_W, GRID_W, d).transpose(0, 2, 1, 3)
    z_l, xbc_l, dt_l = _inproj1(x2, x2v, sc1, sh1, wz, wxbc, wd, cw8, cb, db, n_lat, latent=True)
    _, xbc_c, dt_c = _inproj1(x2, x2v, sc1, sh1, wz, wxbc, wd, cw8, cb, db, n_lat, latent=False)
    a_neg = -jnp.exp(m2_a_log[0].astype(F32))
    lanes = jnp.arange(128)
    heads = jnp.arange(M2_INNER) // M2_HEAD_DIM
    s0 = jnp.zeros((b_, M2_GROUPS, M2_STATE, M2_HPG * M2_HEAD_DIM), F32)
    ys = []
    for dirn in range(2):
        rev = dirn == 1
        arow = jnp.zeros((128,), F32).at[dirn * M2_HEADS:(dirn + 1) * M2_HEADS].set(a_neg[dirn]).reshape(1, 128)
        emat = (lanes[:, None] == heads[None, :] + dirn * M2_HEADS).astype(BF16)
        (st,) = _ssd(xbc_c, dt_c, arow, emat, s0, rev, need_y=False)
        y, _ = _ssd(xbc_l, dt_l, arow, emat, st, rev, need_y=True)
        ys.append(y)
    dx = jnp.repeat(m2_d[0].astype(F32), M2_HEAD_DIM).reshape(1, M2_INNER)
    x3 = _fin1(ys[0], ys[1], z_l, xbc_l, x2v, g1, dx,
               m2_norm[0].astype(F32).reshape(1, M2_INNER), m2_w_out[0].astype(BF16), n_lat)
    out = _moe(x3, sc2, sh2, g2, rwh, rwl, rb, ew1, ew3, ew2, 1,
               norm_final.astype(F32).reshape(1, d), n_lat, final=True)
    return out.reshape(b_, GRID_W, grows, d).transpose(0, 2, 1, 3).reshape(b_, n_lat, d)
```

```python
import functools
import math

import jax
import jax.numpy as jnp
from jax import lax
from jax.experimental import pallas as pl
from jax.experimental.pallas import tpu as pltpu

F32 = jnp.float32
BF16 = jnp.bfloat16
I32 = jnp.int32

D_MODEL = 1024
GRID_W = 64
EPS = 1e-6

S5_WIDTH = 512
S5_GROUP = 16
S5_GROUPS = 32
S5_STATE = 64
RG_WIDTH = 512
RG_HEADS = 8
RG_HEAD_DIM = 64
RG_C = 8.0
CONV_K = 4

M2_INNER = 2048
M2_HEAD_DIM = 64
M2_HEADS = 32
M2_GROUPS = 8
M2_HPG = 4
M2_STATE = 128
M2_CHUNK = 128
M2_CONV_DIM = M2_INNER + 2 * M2_GROUPS * M2_STATE

N_EXPERTS = 16
N_EXPERT_GROUPS = 4
EXPERTS_PER_GROUP = 4
N_CLASSES = 24
MOE_BLOCK = 256
ROW_W = D_MODEL + 128

TOK_TILE = 256
INPROJ1_COLS = 4
HALO = 8
VMEM_LIMIT = 52 * 1024 * 1024


def _cp(sem, vmem=VMEM_LIMIT):
    return pltpu.CompilerParams(dimension_semantics=sem, vmem_limit_bytes=vmem)


def _dot(a, b):
    return jnp.dot(a, b, preferred_element_type=F32)


def _split2(a):
    hi = a.astype(BF16)
    lo = (a - hi.astype(F32)).astype(BF16)
    return hi, lo


def _split3(a):
    a1 = a.astype(BF16)
    r1 = a - a1.astype(F32)
    a2 = r1.astype(BF16)
    a3 = (r1 - a2.astype(F32)).astype(BF16)
    return a1, a2, a3


def _dot3(a, b):
    ah, al = _split2(a)
    bh, bl = _split2(b)
    return _dot(ah, bh) + (_dot(al, bh) + _dot(ah, bl))


def _norm_mod(x, scale, shift):
    ms = jnp.mean(x * x, axis=-1, keepdims=True)
    return (x * lax.rsqrt(ms + EPS)) * scale + shift


def _silu(x):
    return x * jax.nn.sigmoid(x)


def _gelu(x):
    c = math.sqrt(2.0 / math.pi)
    return 0.5 * x * (1.0 + jnp.tanh(c * (x + 0.044715 * (x * x * x))))


def _softplus(x):
    return jnp.maximum(x, 0.0) + jnp.log(1.0 + jnp.exp(-jnp.abs(x)))


def _mod_kernel(c_ref, w_ref, b_ref, o_ref):
    c = c_ref[...]
    o_ref[...] = _dot3(_silu(c), w_ref[...]) + b_ref[...]


def _modulation(c8, w_mod, b_mod):
    depth, d, n = w_mod.shape
    tn = 1536
    return pl.pallas_call(
        _mod_kernel,
        grid=(depth, n // tn),
        in_specs=[pl.BlockSpec((8, d), lambda l, j: (0, 0)),
                  pl.BlockSpec((None, d, tn), lambda l, j: (l, 0, j)),
                  pl.BlockSpec((None, 1, tn), lambda l, j: (l, 0, j))],
        out_specs=pl.BlockSpec((None, 8, tn), lambda l, j: (l, 0, j)),
        out_shape=jax.ShapeDtypeStruct((depth, 8, n), F32),
        compiler_params=_cp(("parallel", "parallel")),
        name="modulation",
    )(c8, w_mod, b_mod.reshape(depth, 1, n))


S5_Q = 16
S5_PAIRS = S5_GROUPS // 2
PAIR_W = 2 * S5_Q * S5_GROUP
LB = 128
NLB = S5_WIDTH // LB


def _inproj0b_kernel(xp_ref, xc_ref, xn_ref, sc_ref, sh_ref, w_ref, cw_ref, cb_ref,
                     u_ref, xrow_ref, xcf_ref, gr_ref, xr_s, us_s, *, nlt):
    i = pl.program_id(1)
    tt = TOK_TILE
    nch = tt // S5_Q
    has_prev = jnp.logical_and(i > 0, i < nlt).astype(F32)
    has_next = (i < nlt - 1).astype(F32)
    xe = jnp.concatenate([xp_ref[...], xc_ref[...], xn_ref[...]], axis=0)
    h = _norm_mod(xe, sc_ref[...], sh_ref[...]).astype(BF16)
    p = _dot(h, w_ref[...])
    u = p[HALO:HALO + tt, 0:S5_WIDTH]
    row = lax.broadcasted_iota(I32, (tt + 2 * HALO, 1), 0)
    keep = jnp.where(row < HALO, has_prev, jnp.where(row >= tt + HALO, has_next, 1.0))
    xr_s[...] = p[:, S5_WIDTH:S5_WIDTH + RG_WIDTH] * keep
    xc = cb_ref[...] + cw_ref[0:1, :] * xr_s[pl.ds(HALO - 1, tt), :]
    for k in range(1, CONV_K):
        xc = xc + cw_ref[k:k + 1, :] * xr_s[pl.ds(HALO - 1 + k, tt), :]
    u_ref[...] = u
    gr_ref[...] = p[HALO:HALO + tt, S5_WIDTH + RG_WIDTH:]
    xcf_ref[...] = xc
    for q in range(NLB):
        us_s[q] = u[:, LB * q:LB * (q + 1)]
    gpb = LB // S5_GROUP
    for q in range(NLB):
        steps = [us_s[q, pl.ds(s, nch, stride=S5_Q), :] for s in range(S5_Q)]
        for m in range(gpb // 2):
            halves = [jnp.concatenate([v[:, S5_GROUP * (2 * m + gl):S5_GROUP * (2 * m + gl + 1)] for v in steps],
                                      axis=1) for gl in range(2)]
            xrow_ref[(gpb // 2) * q + m] = jnp.concatenate(halves, axis=1).astype(BF16)


def _inproj0b(xall, sc1, sh1, w_in, cw8, cb, n_lat):
    b_, nt, d = xall.shape
    tt = TOK_TILE
    nch = tt // S5_Q
    nlt = n_lat // tt
    nti = nt // tt
    hb = tt // HALO
    nh = nt // HALO
    row_idx = lambda b, i: (jnp.where(i < nlt, b, b_), 0, 0)
    tok = lambda w: pl.BlockSpec((None, tt, w), lambda b, i: (b, i, 0))
    kern = functools.partial(_inproj0b_kernel, nlt=nlt)
    return pl.pallas_call(
        kern,
        grid=(b_, nti),
        in_specs=[
            pl.BlockSpec((None, HALO, d), lambda b, i: (b, jnp.maximum(i * hb - 1, 0), 0)),
            pl.BlockSpec((None, tt, d), lambda b, i: (b, i, 0)),
            pl.BlockSpec((None, HALO, d), lambda b, i: (b, jnp.minimum(i * hb + hb, nh - 1), 0)),
            pl.BlockSpec((None, 1, d), row_idx),
            pl.BlockSpec((None, 1, d), row_idx),
            pl.BlockSpec(w_in.shape, lambda b, i: (0, 0)),
            pl.BlockSpec((8, RG_WIDTH), lambda b, i: (0, 0)),
            pl.BlockSpec((1, RG_WIDTH), lambda b, i: (0, 0)),
        ],
        out_specs=[
            tok(S5_WIDTH),
            pl.BlockSpec((S5_PAIRS, nch, PAIR_W), lambda b, i: (0, i * b_ + b, 0)),
            tok(RG_WIDTH),
            tok(RG_WIDTH),
        ],
        out_shape=[
            jax.ShapeDtypeStruct((b_, nt, S5_WIDTH), F32),
            jax.ShapeDtypeStruct((S5_PAIRS, b_ * nt // S5_Q, PAIR_W), BF16),
            jax.ShapeDtypeStruct((b_, nt, RG_WIDTH), F32),
            jax.ShapeDtypeStruct((b_, nt, RG_WIDTH), F32),
        ],
        scratch_shapes=[pltpu.VMEM((tt + 2 * HALO, RG_WIDTH), F32), pltpu.VMEM((NLB, tt, LB), F32)],
        compiler_params=_cp(("parallel", "parallel")),
        name="inproj0",
    )(xall, xall, xall, sc1, sh1, w_in, cw8, cb)


def _s5_local_kernel(x_ref, t_ref, wo_ref, yx_ref, fr_ref, fi_ref, br_ref, bi_ref):
    x = x_ref[...]
    yx_ref[...] = _dot(x, t_ref[...])
    sx = _dot(x, wo_ref[...])
    fr_ref[...] = sx[:, 0:LB]
    fi_ref[...] = sx[:, LB:2 * LB]
    br_ref[...] = sx[:, 2 * LB:3 * LB]
    bi_ref[...] = sx[:, 3 * LB:4 * LB]


def _s5_local(xrow, tmat, wout):
    npair, r, _ = xrow.shape
    pm = lambda w: pl.BlockSpec((None, r, w), lambda p: (p, 0, 0))
    wm = pl.BlockSpec((None, PAIR_W, PAIR_W), lambda p: (p, 0, 0))
    st = jax.ShapeDtypeStruct((npair, r, LB), F32)
    return pl.pallas_call(
        _s5_local_kernel,
        grid=(npair,),
        in_specs=[pm(PAIR_W), wm, wm],
        out_specs=[pm(PAIR_W), pm(LB), pm(LB), pm(LB), pm(LB)],
        out_shape=[jax.ShapeDtypeStruct((npair, r, PAIR_W), F32), st, st, st, st],
        compiler_params=_cp(("parallel",)),
        name="s5_local",
    )(xrow, tmat, wout)


def _s5_state_out_kernel(hr_ref, hi_ref, gr_ref, gi_ref, win_ref, yx_ref, o_ref):
    hg = jnp.concatenate([hr_ref[...], hi_ref[...], gr_ref[...], gi_ref[...]], axis=1).astype(BF16)
    o_ref[...] = yx_ref[...] + _dot(hg, win_ref[...])


def _s5_state_out(hr, hi, gr, gi, win, yx):
    npair, r, _ = yx.shape
    pm = lambda w: pl.BlockSpec((None, r, w), lambda p: (p, 0, 0))
    return pl.pallas_call(
        _s5_state_out_kernel,
        grid=(npair,),
        in_specs=[pm(LB), pm(LB), pm(LB), pm(LB), pl.BlockSpec((None, PAIR_W, PAIR_W), lambda p: (p, 0, 0)),
                  pm(PAIR_W)],
        out_specs=pm(PAIR_W),
        out_shape=jax.ShapeDtypeStruct((npair, r, PAIR_W), F32),
        input_output_aliases={5: 0},
        compiler_params=_cp(("parallel",)),
        name="s5_state_out",
    )(hr, hi, gr, gi, win, yx)


def _scan0b_kernel(fr_ref, fi_ref, br_ref, bi_ref, xcf_ref, xcb_ref, aq_ref, wa_ref, wx_ref, ba_ref, bx_ref,
                   sp_ref, hr_ref, hi_ref, gr_ref, gi_ref, hgf_ref, hgb_ref, st_s, ga_s, gb_s, hg_s):
    j = pl.program_id(0)
    nb, tt, _ = xcf_ref.shape
    nch = tt // S5_Q

    @pl.when(j == 0)
    def _():
        st_s[...] = jnp.zeros_like(st_s)
        hg_s[...] = jnp.zeros_like(hg_s)

    for d, (sr_ref, si_ref, or_ref, oi_ref) in enumerate(((fr_ref, fi_ref, hr_ref, hi_ref),
                                                          (br_ref, bi_ref, gr_ref, gi_ref))):
        for k in range(nch):
            c = k if d == 0 else nch - 1 - k
            rows = pl.ds(c, nb, stride=nch)
            for p in range(S5_PAIRS):
                hr = st_s[d, 0, p]
                hi = st_s[d, 1, p]
                or_ref[p, rows, :] = hr
                oi_ref[p, rows, :] = hi
                ar = aq_ref[d, 0, p]
                ai = aq_ref[d, 1, p]
                st_s[d, 0, p] = ar * hr - ai * hi + sr_ref[p, rows, :]
                st_s[d, 1, p] = ar * hi + ai * hr + si_ref[p, rows, :]

    for d, x_ref in enumerate((xcf_ref, xcb_ref)):
        for b in range(nb):
            x = x_ref[b]
            xb = x.astype(BF16)
            log_a = (-RG_C) * jax.nn.sigmoid(_dot(xb, wa_ref[d]) + ba_ref[d]) * sp_ref[d]
            bterm = jnp.sqrt(1.0 - jnp.exp(2.0 * log_a)) * jax.nn.sigmoid(_dot(xb, wx_ref[d]) + bx_ref[d]) * x
            a = jnp.exp(log_a)
            for q in range(NLB):
                ga_s[d, q, b * tt:(b + 1) * tt, :] = a[:, LB * q:LB * (q + 1)]
                gb_s[d, q, b * tt:(b + 1) * tt, :] = bterm[:, LB * q:LB * (q + 1)]

    def step(k, carry):
        out = []
        for d in range(2):
            t = k if d == 0 else tt - 1 - k
            rows = pl.ds(t, nb, stride=tt)
            for q in range(NLB):
                hnew = ga_s[d, q, rows, :] * carry[d * NLB + q] + gb_s[d, q, rows, :]
                gb_s[d, q, rows, :] = hnew
                out.append(hnew)
        return tuple(out)

    init = tuple(hg_s[d, q] for d in range(2) for q in range(NLB))
    fin = lax.fori_loop(0, tt, step, init)
    for d in range(2):
        for q in range(NLB):
            hg_s[d, q] = fin[d * NLB + q]
    for d, o_ref in enumerate((hgf_ref, hgb_ref)):
        for b in range(nb):
            o_ref[b] = jnp.concatenate([gb_s[d, q, b * tt:(b + 1) * tt, :] for q in range(NLB)], axis=1)


def _scan0b(sfr, sfi, sbr, sbi, xc, aq, wa, wx, ba, bx, sp, n_lat):
    b_, nt, _ = xc.shape
    tt = TOK_TILE
    nch = tt // S5_Q
    ntile = nt // tt
    nlt = n_lat // tt
    nct = ntile - nlt
    tf = lambda j: jnp.where(j < nct, nlt + j, j - nct)
    tb = lambda j: ntile - 1 - j
    rblk = b_ * nch
    sf = pl.BlockSpec((S5_PAIRS, rblk, LB), lambda j: (0, tf(j), 0))
    sb = pl.BlockSpec((S5_PAIRS, rblk, LB), lambda j: (0, tb(j), 0))
    xf = pl.BlockSpec((b_, tt, RG_WIDTH), lambda j: (0, tf(j), 0))
    xb = pl.BlockSpec((b_, tt, RG_WIDTH), lambda j: (0, tb(j), 0))
    full = lambda a: pl.BlockSpec(a.shape, lambda j: (0,) * a.ndim)
    st = jax.ShapeDtypeStruct(sfr.shape, F32)
    hg = jax.ShapeDtypeStruct(xc.shape, F32)
    return pl.pallas_call(
        _scan0b_kernel,
        grid=(ntile,),
        in_specs=[sf, sf, sb, sb, xf, xb, full(aq), full(wa), full(wx), full(ba), full(bx), full(sp)],
        out_specs=[sf, sf, sb, sb, xf, xb],
        out_shape=[st, st, st, st, hg, hg],
        scratch_shapes=[
            pltpu.VMEM((2, 2, S5_PAIRS, b_, LB), F32),
            pltpu.VMEM((2, NLB, b_ * tt, LB), F32),
            pltpu.VMEM((2, NLB, b_ * tt, LB), F32),
            pltpu.VMEM((2, NLB, b_, LB), F32),
        ],
        compiler_params=_cp(("arbitrary",)),
        name="scan0",
    )(sfr, sfi, sbr, sbi, xc, xc, aq, wa, wx, ba, bx, sp)


def _merge0b_kernel(yx_ref, hf_ref, hb_ref, u_ref, gr_ref, x_ref, g1_ref, d_ref, wglu_ref, wout_ref, o_ref, ys_s):
    nch = yx_ref.shape[1]
    gpb = LB // S5_GROUP
    rows = [yx_ref[p] for p in range(S5_PAIRS)]
    for t in range(S5_Q):
        for q in range(NLB):
            pieces = []
            for m in range(gpb // 2):
                r = rows[(gpb // 2) * q + m]
                for gl in range(2):
                    o = gl * S5_Q * S5_GROUP + t * S5_GROUP
                    pieces.append(r[:, o:o + S5_GROUP])
            ys_s[q, pl.ds(t, nch, stride=S5_Q), :] = jnp.concatenate(pieces, axis=1)
    y = jnp.concatenate([ys_s[q] for q in range(NLB)], axis=1) + d_ref[...] * u_ref[...]
    y = _gelu(y)
    y = y * jax.nn.sigmoid(_dot(y.astype(BF16), wglu_ref[...]))
    g = (hf_ref[...] + hb_ref[...]) * _gelu(gr_ref[...])
    m = _dot(jnp.concatenate([y, g], axis=1).astype(BF16), wout_ref[...])
    o_ref[...] = x_ref[...] + g1_ref[...] * m


def _merge0b(yx, hgf, hgb, u, gr, xall, g1, s5d, wglu, wout, n_lat):
    b_, nt, d = xall.shape
    tt = TOK_TILE
    nch = tt // S5_Q
    nlt = n_lat // tt
    row_idx = lambda b, i: (jnp.where(i < nlt, b, b_), 0, 0)
    tok = lambda w: pl.BlockSpec((None, tt, w), lambda b, i: (b, i, 0))
    const = lambda b, i: (0, 0)
    return pl.pallas_call(
        _merge0b_kernel,
        grid=(b_, nt // tt),
        in_specs=[
            pl.BlockSpec((S5_PAIRS, nch, PAIR_W), lambda b, i: (0, i * b_ + b, 0)),
            tok(RG_WIDTH), tok(RG_WIDTH), tok(S5_WIDTH), tok(RG_WIDTH), tok(d),
            pl.BlockSpec((None, 1, d), row_idx),
            pl.BlockSpec((1, S5_WIDTH), const),
            pl.BlockSpec(wglu.shape, const),
            pl.BlockSpec(wout.shape, const),
        ],
        out_specs=tok(d),
        out_shape=jax.ShapeDtypeStruct((b_, nt, d), F32),
        scratch_shapes=[pltpu.VMEM((NLB, tt, LB), F32)],
        compiler_params=_cp(("parallel", "parallel")),
        name="merge0",
    )(yx, hgf, hgb, u, gr, xall, g1, s5d, wglu, wout)


def _s5_chunk_params(lam_re, lam_im, log_dt, b_re, b_im, c_re, c_im, nb):
    hp = lax.Precision.HIGHEST
    qn = S5_Q
    lam_re = lam_re.astype(F32)
    lam_im = lam_im.astype(F32)
    dt = jnp.exp(log_dt.astype(F32))[..., None]
    n = jnp.arange(qn + 1, dtype=F32)[:, None, None, None]
    mag = jnp.exp(n * (lam_re * dt)[None])
    ang = n * (lam_im * dt)[None]
    zr = mag * jnp.cos(ang)
    zi = mag * jnp.sin(ang)
    ar, ai = zr[1], zi[1]
    den = lam_re * lam_re + lam_im * lam_im
    fr = ((ar - 1.0) * lam_re + ai * lam_im) / den
    fi = (ai * lam_re - (ar - 1.0) * lam_im) / den
    bbr = fr[..., None] * b_re - fi[..., None] * b_im
    bbi = fr[..., None] * b_im + fi[..., None] * b_re
    cr = c_re.astype(F32)
    ci = c_im.astype(F32)
    wr = zr[:qn, ..., None] * bbr[None] - zi[:qn, ..., None] * bbi[None]
    wi = zr[:qn, ..., None] * bbi[None] + zi[:qn, ..., None] * bbr[None]
    kern = (jnp.einsum('dgkp,ndgpj->ndgkj', cr, wr, precision=hp)
            - jnp.einsum('dgkp,ndgpj->ndgkj', ci, wi, precision=hp))
    s = jnp.arange(qn)
    lag = s[None, :] - s[:, None]
    tf = jnp.where((lag >= 0)[:, :, None, None, None], kern[jnp.clip(lag, 0, qn - 1), 0], 0.0)
    tb = jnp.where((lag <= 0)[:, :, None, None, None], kern[jnp.clip(-lag, 0, qn - 1), 1], 0.0)
    tg = (tf + tb).transpose(2, 0, 4, 1, 3).reshape(S5_GROUPS, qn * S5_GROUP, qn * S5_GROUP)

    def pair_diag(m):
        g, r, c = m.shape
        m = m.reshape(g // 2, 2, r, c)
        z = jnp.zeros((g // 2, 2, r, 2, c), F32)
        z = z.at[:, 0, :, 0, :].set(m[:, 0]).at[:, 1, :, 1, :].set(m[:, 1])
        return z.reshape(g // 2, 2 * r, 2 * c)

    tmat = pair_diag(tg).astype(BF16)
    def out_part(w, d, flip):
        m = w[::-1, d] if flip else w[:, d]
        return m.transpose(1, 0, 3, 2).reshape(S5_GROUPS, qn * S5_GROUP, S5_STATE)
    parts = [out_part(wr, 0, True), out_part(wi, 0, True), out_part(wr, 1, False), out_part(wi, 1, False)]
    wout = jnp.concatenate([pair_diag(pp) for pp in parts], axis=2).astype(BF16)
    def in_parts(d, powers):
        zr_p = zr[powers, d]
        zi_p = zi[powers, d]
        m_re = (cr[d][None] * zr_p[:, :, None, :] - ci[d][None] * zi_p[:, :, None, :])
        m_im = -(cr[d][None] * zi_p[:, :, None, :] + ci[d][None] * zr_p[:, :, None, :])
        f = lambda m: m.transpose(1, 3, 0, 2).reshape(S5_GROUPS, S5_STATE, qn * S5_GROUP)
        return f(m_re), f(m_im)
    fre, fim = in_parts(0, jnp.arange(1, qn + 1))
    bre, bim = in_parts(1, qn - jnp.arange(qn))
    win = jnp.concatenate([pair_diag(pp) for pp in (fre, fim, bre, bim)], axis=1).astype(BF16)
    aq = jnp.stack([zr[qn], zi[qn]], axis=1).reshape(2, 2, S5_PAIRS, 1, LB)
    return tmat, wout, win, jnp.broadcast_to(aq, (2, 2, S5_PAIRS, nb, LB))


def _rg_params_b(wa, ba, wx, bx, lam):
    eye = jnp.eye(RG_HEADS, dtype=F32)
    dense = lambda w: jnp.einsum('dhij,hk->dhikj', w.astype(F32), eye).reshape(2, RG_WIDTH, RG_WIDTH).astype(BF16)
    row = lambda v: v.astype(F32).reshape(2, 1, RG_WIDTH)
    return dense(wa), dense(wx), row(ba), row(bx), row(jax.nn.softplus(-lam.astype(F32)))


def _route_kernel(x_ref, sc_ref, sh_ref, rwh_ref, rwl_ref, rb_ref, hb_ref, meta_ref, cnt_ref, carry_s):
    first = jnp.logical_and(pl.program_id(0) == 0, pl.program_id(1) == 0)

    @pl.when(first)
    def _():
        carry_s[...] = jnp.zeros_like(carry_s)

    tt = TOK_TILE
    h = _norm_mod(x_ref[...], sc_ref[...], sh_ref[...])
    hh, hl = _split2(h)
    logits = _dot(hh, rwh_ref[...]) + (_dot(hl, rwh_ref[...]) + _dot(hh, rwl_ref[...]))
    lane = lax.broadcasted_iota(I32, (tt, 128), 1)
    valid = lane < N_EXPERTS
    neg = -1e30
    lg = jnp.where(valid, logits, neg)
    mx = jnp.max(lg, axis=-1, keepdims=True)
    ex = jnp.where(valid, jnp.exp(lg - mx), 0.0)
    probs = ex / jnp.sum(ex, axis=-1, keepdims=True)
    sel = probs + rb_ref[...]
    grp = jnp.right_shift(lane, 2)
    best = jnp.zeros((tt, 1), I32)
    bestv = jnp.max(jnp.where(jnp.logical_and(valid, grp == 0), sel, neg), axis=-1, keepdims=True)
    for k in range(1, N_EXPERT_GROUPS):
        gk = jnp.max(jnp.where(jnp.logical_and(valid, grp == k), sel, neg), axis=-1, keepdims=True)
        upd = gk > bestv
        best = jnp.where(upd, k, best)
        bestv = jnp.where(upd, gk, bestv)
    msel = jnp.where(jnp.logical_and(valid, grp == best), sel, neg)
    v1 = jnp.max(msel, axis=-1, keepdims=True)
    i1 = jnp.min(jnp.where(msel == v1, lane, 128), axis=-1, keepdims=True)
    msel2 = jnp.where(lane == i1, neg, msel)
    v2 = jnp.max(msel2, axis=-1, keepdims=True)
    i2 = jnp.min(jnp.where(msel2 == v2, lane, 128), axis=-1, keepdims=True)
    p1 = jnp.sum(jnp.where(lane == i1, probs, 0.0), axis=-1, keepdims=True)
    p2 = jnp.sum(jnp.where(lane == i2, probs, 0.0), axis=-1, keepdims=True)
    den = p1 + p2
    g1 = p1 / den
    g2 = p2 / den
    lo = jnp.minimum(i1, i2) - EXPERTS_PER_GROUP * best
    hi = jnp.maximum(i1, i2) - EXPERTS_PER_GROUP * best
    g_lo = jnp.where(i1 < i2, g1, g2)
    g_hi = jnp.where(i1 < i2, g2, g1)
    pair = jnp.right_shift(lo * (7 - lo), 1) + (hi - lo - 1)
    cls = best * 6 + pair

    onehot = jnp.where(lane == cls, 1.0, 0.0)
    r_i = lax.broadcasted_iota(I32, (tt, tt), 0)
    c_i = lax.broadcasted_iota(I32, (tt, tt), 1)
    before = jnp.where(c_i < r_i, 1.0, 0.0).astype(BF16)
    cum = _dot(before, onehot.astype(BF16))
    rank = jnp.sum(onehot * (cum + carry_s[...]), axis=-1, keepdims=True)
    carry_s[...] = carry_s[...] + jnp.sum(onehot, axis=0, keepdims=True)

    hb_ref[:, 0:D_MODEL] = h
    hb_ref[:, D_MODEL:] = jnp.where(lane == 0, g_lo, jnp.where(lane == 1, g_hi, 0.0))
    meta = jnp.where(lane == 0, cls.astype(F32), jnp.where(lane == 1, rank, 0.0))
    meta_ref[...] = meta.T[0:8, :].astype(I32)
    cnt_ref[...] = jnp.broadcast_to(carry_s[...], (8, 128))


def _route(x, sc2, sh2, rwh, rwl, rb, n_lat):
    b_, nt, d = x.shape
    tt = TOK_TILE
    nlt = n_lat // tt
    nti = nt // tt
    row_idx = lambda b, i: (jnp.where(i < nlt, b, b_), 0, 0)
    flat = lambda b, i: (b * nti + i, 0)
    return pl.pallas_call(
        _route_kernel,
        grid=(b_, nti),
        in_specs=[
            pl.BlockSpec((None, tt, d), lambda b, i: (b, i, 0)),
            pl.BlockSpec((None, 1, d), row_idx),
            pl.BlockSpec((None, 1, d), row_idx),
            pl.BlockSpec((d, 128), lambda b, i: (0, 0)),
            pl.BlockSpec((d, 128), lambda b, i: (0, 0)),
            pl.BlockSpec((1, 128), lambda b, i: (0, 0)),
        ],
        out_specs=[
            pl.BlockSpec((tt, ROW_W), flat),
            pl.BlockSpec((None, 8, tt), lambda b, i: (b * nti + i, 0, 0)),
            pl.BlockSpec((8, 128), lambda b, i: (0, 0)),
        ],
        out_shape=[
            jax.ShapeDtypeStruct((b_ * nt, ROW_W), F32),
            jax.ShapeDtypeStruct((b_ * nti, 8, tt), I32),
            jax.ShapeDtypeStruct((8, 128), F32),
        ],
        scratch_shapes=[pltpu.VMEM((1, 128), F32)],
        compiler_params=_cp(("arbitrary", "arbitrary")),
        name="route",
    )(x, sc2, sh2, rwh, rwl, rb)


def _row_copy(src, dst, src_row, dst_row, sem):
    return pltpu.make_async_copy(src.at[pl.ds(src_row, 1)], dst.at[pl.ds(dst_row, 1)], sem)


def _dest_row(ps_ref, meta_ref, r):
    return ps_ref[meta_ref[0, r]] + meta_ref[1, r]


ISSUE_UNROLL = 8


def _issue_rows(n, start_row):
    def body(j, carry):
        for k in range(ISSUE_UNROLL):
            start_row(j * ISSUE_UNROLL + k, k % 2)
        return carry

    lax.fori_loop(0, n // ISSUE_UNROLL, body, 0)


def _dispatch_kernel(ps_ref, meta_ref, hb_ref, xb_in, xb_out, ring, fsems, sems):
    del xb_in
    tt = TOK_TILE
    i = pl.program_id(0)
    n = pl.num_programs(0)
    slot = lax.rem(i, 2)
    rslot = lax.rem(i, 3)

    def fetch(t, s):
        return pltpu.make_async_copy(hb_ref.at[pl.ds(t * tt, tt)], ring.at[s], fsems.at[s])

    @pl.when(i == 0)
    def _():
        fetch(0, 0).start()

    @pl.when(i + 1 < n)
    def _():
        fetch(i + 1, lax.rem(i + 1, 3)).start()

    fetch(i, rslot).wait()
    src = ring.at[rslot]

    def start_row(r, prio):
        _row_copy(src, xb_out, r, _dest_row(ps_ref, meta_ref, r), sems.at[slot]).start(priority=prio)

    _issue_rows(tt, start_row)

    def drain(s):
        pltpu.make_async_copy(src, xb_out.at[pl.ds(0, tt)], sems.at[s]).wait()

    @pl.when(i > 0)
    def _():
        drain(1 - slot)

    @pl.when(i == n - 1)
    def _():
        drain(slot)


def _dispatch(pstart, meta, hb, n_rows):
    t = hb.shape[0]
    tt = TOK_TILE
    zeros = jnp.zeros((n_rows, ROW_W), F32)
    gs = pltpu.PrefetchScalarGridSpec(
        num_scalar_prefetch=1,
        grid=(t // tt,),
        in_specs=[
            pl.BlockSpec((None, 8, tt), lambda i, ps: (i, 0, 0), memory_space=pltpu.SMEM),
            pl.BlockSpec(memory_space=pl.ANY),
            pl.BlockSpec(memory_space=pl.ANY),
        ],
        out_specs=pl.BlockSpec(memory_space=pl.ANY),
        scratch_shapes=[pltpu.VMEM((3, tt, ROW_W), F32), pltpu.SemaphoreType.DMA((3,)),
                        pltpu.SemaphoreType.DMA((2,))],
    )
    return pl.pallas_call(
        _dispatch_kernel,
        grid_spec=gs,
        out_shape=jax.ShapeDtypeStruct((n_rows, ROW_W), F32),
        input_output_aliases={3: 0},
        compiler_params=_cp(("arbitrary",)),
        name="dispatch",
    )(pstart, meta, hb, zeros)


def _expert_kernel(ea_ref, eb_ref, nu_ref, x_ref, w1a, w1b, w3a, w3b, w2a, w2b, o_ref):
    del ea_ref, eb_ref
    j = pl.program_id(0)

    @pl.when(j < nu_ref[0])
    def _():
        x = x_ref[:, 0:D_MODEL].astype(BF16)
        g_lo = x_ref[:, D_MODEL:D_MODEL + 1]
        g_hi = x_ref[:, D_MODEL + 1:D_MODEL + 2]

        def ffn(w1, w3, w2):
            a = _dot(x, w1[...])
            b = _dot(x, w3[...])
            return _dot((_silu(a) * b).astype(BF16), w2[...])

        o_ref[...] = g_lo * ffn(w1a, w3a, w2a) + g_hi * ffn(w1b, w3b, w2b)

    @pl.when(j >= nu_ref[0])
    def _():
        o_ref[...] = jnp.zeros_like(o_ref)


def _experts(blk_ea, blk_eb, nused, xb, w1, w3, w2, layer):
    n_rows = xb.shape[0]
    nb = n_rows // MOE_BLOCK
    d = D_MODEL
    wa = lambda j, ea, eb, nu: (layer, ea[j], 0, 0)
    wb = lambda j, ea, eb, nu: (layer, eb[j], 0, 0)
    wspec = lambda f: pl.BlockSpec((None, None, d, d), f)
    gs = pltpu.PrefetchScalarGridSpec(
        num_scalar_prefetch=3,
        grid=(nb,),
        in_specs=[pl.BlockSpec((MOE_BLOCK, ROW_W), lambda j, ea, eb, nu: (j, 0)),
                  wspec(wa), wspec(wb), wspec(wa), wspec(wb), wspec(wa), wspec(wb)],
        out_specs=pl.BlockSpec((MOE_BLOCK, d), lambda j, ea, eb, nu: (j, 0)),
    )
    return pl.pallas_call(
        _expert_kernel,
        grid_spec=gs,
        out_shape=jax.ShapeDtypeStruct((n_rows, d), F32),
        compiler_params=_cp(("arbitrary",)),
        name="experts",
    )(blk_ea, blk_eb, nused, xb, w1, w1, w3, w3, w2, w2)


def _combine_kernel(ps_ref, meta_ref, metan_ref, yb_ref, x_ref, g2_ref, nw_ref, o_ref, buf, sems, *, final):
    tt = TOK_TILE
    i = pl.program_id(0)
    n = pl.num_programs(0)
    slot = lax.rem(i, 2)

    def gather(mref, s):
        def start_row(r, prio):
            _row_copy(yb_ref, buf.at[s], _dest_row(ps_ref, mref, r), r, sems.at[s]).start(priority=prio)

        _issue_rows(tt, start_row)

    @pl.when(i == 0)
    def _():
        gather(meta_ref, 0)

    @pl.when(i + 1 < n)
    def _():
        gather(metan_ref, 1 - slot)

    pltpu.make_async_copy(yb_ref.at[pl.ds(0, tt)], buf.at[slot], sems.at[slot]).wait()
    x2 = x_ref[...] + g2_ref[...] * buf[slot]
    if final:
        ms = jnp.mean(x2 * x2, axis=-1, keepdims=True)
        x2 = x2 * lax.rsqrt(ms + EPS) * nw_ref[...]
    o_ref[...] = x2


def _combine(pstart, meta, yb, x, g2, nw, n_lat, final):
    b_, nt, d = x.shape
    tt = TOK_TILE
    nlt = n_lat // tt
    nti = nt // tt
    ntile = b_ * nti
    row_idx = lambda i, ps: (jnp.where(lax.rem(i, nti) < nlt, i // nti, b_), 0, 0)
    kern = functools.partial(_combine_kernel, final=final)
    gs = pltpu.PrefetchScalarGridSpec(
        num_scalar_prefetch=1,
        grid=(ntile,),
        in_specs=[
            pl.BlockSpec((None, 8, tt), lambda i, ps: (i, 0, 0), memory_space=pltpu.SMEM),
            pl.BlockSpec((None, 8, tt), lambda i, ps: (jnp.minimum(i + 1, ntile - 1), 0, 0),
                         memory_space=pltpu.SMEM),
            pl.BlockSpec(memory_space=pl.ANY),
            pl.BlockSpec((tt, d), lambda i, ps: (i, 0)),
            pl.BlockSpec((None, 1, d), row_idx),
            pl.BlockSpec((1, d), lambda i, ps: (0, 0)),
        ],
        out_specs=pl.BlockSpec((tt, d), lambda i, ps: (i, 0)),
        scratch_shapes=[pltpu.VMEM((2, tt, d), F32), pltpu.SemaphoreType.DMA((2,))],
    )
    out = pl.pallas_call(
        kern,
        grid_spec=gs,
        out_shape=jax.ShapeDtypeStruct((b_ * nt, d), F32),
        compiler_params=_cp(("arbitrary",)),
        name="combine",
    )(pstart, meta, meta, yb, x.reshape(b_ * nt, d), g2, nw)
    return out.reshape(b_, nt, d)


def _class_tables():
    ea, eb = [], []
    for g in range(N_EXPERT_GROUPS):
        for lo in range(EXPERTS_PER_GROUP):
            for hi in range(lo + 1, EXPERTS_PER_GROUP):
                ea.append(g * EXPERTS_PER_GROUP + lo)
                eb.append(g * EXPERTS_PER_GROUP + hi)
    return jnp.array(ea, I32), jnp.array(eb, I32)


def _moe(x, sc2, sh2, g2, rwh, rwl, rb, w1, w3, w2, layer, nw, n_lat, final):
    b_, nt, d = x.shape
    t = b_ * nt
    hb, meta, cnt = _route(x, sc2, sh2, rwh, rwl, rb, n_lat)
    counts = cnt[0, :N_CLASSES].astype(I32)
    padded = (counts + MOE_BLOCK - 1) // MOE_BLOCK * MOE_BLOCK
    pend = jnp.cumsum(padded)
    pstart = pend - padded
    nb = t // MOE_BLOCK + N_CLASSES
    blk_start = jnp.arange(nb, dtype=I32)[:, None] * MOE_BLOCK
    blk_cls = jnp.minimum(jnp.sum((pend[None, :] <= blk_start).astype(I32), axis=1), N_CLASSES - 1)
    ea, eb = _class_tables()
    nused = (pend[-1] // MOE_BLOCK).reshape(1).astype(I32)
    xb = _dispatch(pstart, meta, hb, nb * MOE_BLOCK)
    yb = _experts(jnp.take(ea, blk_cls), jnp.take(eb, blk_cls), nused, xb, w1, w3, w2, layer)
    return _combine(pstart, meta, yb, x, g2, nw, n_lat, final)


def _inproj1_kernel(*refs, latent, nq):
    if latent:
        (xp_ref, xm_ref, xn_ref, sc_ref, sh_ref, wz_ref, wx_ref, wd_ref, cw_ref, cb_ref, db_ref,
         z_ref, xbc_ref, dt_ref, scr) = refs
        q = pl.program_id(1)
        has_prev = (q > 0).astype(F32)
        has_next = (q < nq - 1).astype(F32)
        cols = [xm_ref[k] for k in range(xm_ref.shape[0])]
        xe = jnp.concatenate([xp_ref[...]] + cols + [xn_ref[...]], axis=0)
    else:
        (xm_ref, sc_ref, sh_ref, wz_ref, wx_ref, wd_ref, cw_ref, cb_ref, db_ref,
         z_ref, xbc_ref, dt_ref, scr) = refs
        has_prev = 0.0
        has_next = 0.0
        pad = jnp.zeros((HALO, D_MODEL), F32)
        xe = jnp.concatenate([pad, xm_ref[...], pad], axis=0)
    tt = z_ref.shape[0]
    h = _norm_mod(xe, sc_ref[...], sh_ref[...]).astype(BF16)
    hc = h[HALO:HALO + tt]
    z_ref[...] = _dot(hc, wz_ref[...]).astype(BF16)
    row = lax.broadcasted_iota(I32, (tt + 2 * HALO, 1), 0)
    keep = jnp.where(row < HALO, has_prev, jnp.where(row >= tt + HALO, has_next, 1.0))
    pw = 1024
    for pc in range(M2_CONV_DIM // pw):
        sl = slice(pc * pw, (pc + 1) * pw)
        scr[...] = _dot(h, wx_ref[:, sl]) * keep
        acc = cb_ref[:, sl] + cw_ref[0:1, sl] * scr[pl.ds(HALO - 1, tt), :]
        for k in range(1, CONV_K):
            acc = acc + cw_ref[k:k + 1, sl] * scr[pl.ds(HALO - 1 + k, tt), :]
        xbc_ref[:, sl] = _silu(acc).astype(BF16)
    lane = lax.broadcasted_iota(I32, (tt, 128), 1)
    dt = _softplus(_dot(hc, wd_ref[...]) + db_ref[...])
    dt_ref[...] = jnp.where(lane < 2 * M2_HEADS, dt, 0.0)


def _inproj1(xall, xcol, sc1, sh1, wz, wx, wd, cw8, cb, db, n_lat, latent):
    b_, nt, d = xall.shape
    if latent:
        n = n_lat
        rows = n_lat // GRID_W
        nc = INPROJ1_COLS
        tt = nc * rows
        nq = GRID_W // nc
        rb = rows // HALO
        x_specs = [
            pl.BlockSpec((None, None, HALO, d), lambda b, q: (b, jnp.maximum(nc * q - 1, 0), rb - 1, 0)),
            pl.BlockSpec((None, nc, rows, d), lambda b, q: (b, q, 0, 0)),
            pl.BlockSpec((None, None, HALO, d), lambda b, q: (b, jnp.minimum(nc * q + nc, GRID_W - 1), 0, 0)),
        ]
        x_args = [xcol, xcol, xcol]
        row_idx = lambda b, q: (b, 0, 0)
    else:
        tt = TOK_TILE
        n = nt - n_lat
        nq = n // tt
        off = n_lat // tt
        x_specs = [pl.BlockSpec((None, tt, d), lambda b, q: (b, off + q, 0))]
        x_args = [xall]
        row_idx = lambda b, q: (b_, 0, 0)
    const = lambda b, q: (0, 0)
    kern = functools.partial(_inproj1_kernel, latent=latent, nq=nq)
    return pl.pallas_call(
        kern,
        grid=(b_, nq),
        in_specs=x_specs + [
            pl.BlockSpec((None, 1, d), row_idx),
            pl.BlockSpec((None, 1, d), row_idx),
            pl.BlockSpec(wz.shape, const, pipeline_mode=pl.Buffered(1)),
            pl.BlockSpec(wx.shape, const, pipeline_mode=pl.Buffered(1)),
            pl.BlockSpec(wd.shape, const, pipeline_mode=pl.Buffered(1)),
            pl.BlockSpec(cw8.shape, const),
            pl.BlockSpec(cb.shape, const),
            pl.BlockSpec(db.shape, const),
        ],
        out_specs=[
            pl.BlockSpec((None, tt, M2_INNER), lambda b, q: (b, q, 0)),
            pl.BlockSpec((None, tt, M2_CONV_DIM), lambda b, q: (b, q, 0)),
            pl.BlockSpec((None, tt, 128), lambda b, q: (b, q, 0)),
        ],
        out_shape=[
            jax.ShapeDtypeStruct((b_, n, M2_INNER), BF16),
            jax.ShapeDtypeStruct((b_, n, M2_CONV_DIM), BF16),
            jax.ShapeDtypeStruct((b_, n, 128), F32),
        ],
        scratch_shapes=[pltpu.VMEM((tt + 2 * HALO, 1024), F32)],
        compiler_params=_cp(("parallel", "parallel")),
        name="inproj1_lat" if latent else "inproj1_ctx",
    )(*x_args, sc1, sh1, wz, wx, wd, cw8, cb, db)


def _ssd_kernel(*refs, reverse, need_y, lane0):
    if need_y:
        xbc_ref, dt_ref, arow_ref, e_ref, s0_ref, y_ref, sf_ref, s_s = refs
    else:
        xbc_ref, dt_ref, arow_ref, e_ref, s0_ref, sf_ref, s_s = refs
    c = pl.program_id(1)
    nc = pl.num_programs(1)
    q = M2_CHUNK

    @pl.when(c == 0)
    def _():
        s_s[...] = s0_ref[...]

    dt = dt_ref[...]
    a = dt * arow_ref[...]
    row = lax.broadcasted_iota(I32, (q, q), 0)
    col = lax.broadcasted_iota(I32, (q, q), 1)
    incl = (col >= row) if reverse else (col <= row)
    incl_t = (row >= col) if reverse else (row <= col)
    lt = jnp.where(incl, 1.0, 0.0).astype(BF16)
    lt_t = jnp.where(incl_t, 1.0, 0.0).astype(BF16)
    a1, a2, a3 = _split3(a)
    cum = _dot(lt, a1) + (_dot(lt, a2) + _dot(lt, a3))
    b1, b2, b3 = _split3(a.T)
    cum_t = _dot(b1, lt_t) + (_dot(b2, lt_t) + _dot(b3, lt_t))
    atot = cum[0:1, :] if reverse else cum[q - 1:q, :]
    wexp = _dot((dt * jnp.exp(atot - cum)).astype(BF16), e_ref[...])
    dt_t = dt.T
    eh, el = _split2(jnp.broadcast_to(jnp.exp(atot), (8, 128)))
    atx = _dot(eh, e_ref[...]) + _dot(el, e_ref[...])

    for g in range(M2_GROUPS):
        bg = xbc_ref[:, M2_INNER + M2_STATE * g:M2_INNER + M2_STATE * (g + 1)]
        gw = M2_HPG * M2_HEAD_DIM
        xg = xbc_ref[:, gw * g:gw * (g + 1)]
        sg = s_s[g]
        if need_y:
            cg = xbc_ref[:, M2_INNER + M2_GROUPS * M2_STATE + M2_STATE * g:
                         M2_INNER + M2_GROUPS * M2_STATE + M2_STATE * (g + 1)]
            cb = lax.dot_general(cg, bg, (((1,), (1,)), ((), ())), preferred_element_type=F32)
            cg32 = cg.astype(F32)
            sgb = sg.astype(BF16)
            y_parts = []
            for r in range(M2_HPG):
                hl = lane0 + M2_HPG * g + r
                ps = slice(M2_HEAD_DIM * r, M2_HEAD_DIM * (r + 1))
                colb = jnp.broadcast_to(cum[:, hl:hl + 1], (q, q))
                rowb = jnp.broadcast_to(cum_t[hl:hl + 1, :], (q, q))
                lmat = jnp.exp(jnp.where(incl, colb - rowb, -1e30))
                gm = (cb * lmat * dt_t[hl:hl + 1, :]).astype(BF16)
                cd = (jnp.exp(colb) * cg32).astype(BF16)
                lhs = jnp.concatenate([gm, cd], axis=1)
                rhs = jnp.concatenate([xg[:, ps], sgb[:, ps]], axis=0)
                y_parts.append(_dot(lhs, rhs))
        xw = (xg.astype(F32) * wexp[:, gw * g:gw * (g + 1)]).astype(BF16)
        snew = lax.dot_general(bg, xw, (((0,), (0,)), ((), ())), preferred_element_type=F32)
        s_s[g] = atx[0:1, gw * g:gw * (g + 1)] * sg + snew
        if need_y:
            y_ref[:, gw * g:gw * (g + 1)] = jnp.concatenate(y_parts, axis=1)

    @pl.when(c == nc - 1)
    def _():
        sf_ref[...] = s_s[...]


def _ssd(xbc, dt, arow, emat, s0, reverse, need_y):
    b_, n, _ = xbc.shape
    q = M2_CHUNK
    nc = n // q
    cidx = (lambda c: nc - 1 - c) if reverse else (lambda c: c)
    sshape = (M2_GROUPS, M2_STATE, M2_HPG * M2_HEAD_DIM)
    kern = functools.partial(_ssd_kernel, reverse=reverse, need_y=need_y, lane0=M2_HEADS if reverse else 0)
    out_specs = [pl.BlockSpec((None,) + sshape, lambda b, c: (b, 0, 0, 0))]
    out_shape = [jax.ShapeDtypeStruct((b_,) + sshape, F32)]
    if need_y:
        out_specs = [pl.BlockSpec((None, q, M2_INNER), lambda b, c: (b, cidx(c), 0))] + out_specs
        out_shape = [jax.ShapeDtypeStruct((b_, n, M2_INNER), F32)] + out_shape
    return pl.pallas_call(
        kern,
        grid=(b_, nc),
        in_specs=[
            pl.BlockSpec((None, q, M2_CONV_DIM), lambda b, c: (b, cidx(c), 0)),
            pl.BlockSpec((None, q, 128), lambda b, c: (b, cidx(c), 0)),
            pl.BlockSpec((1, 128), lambda b, c: (0, 0)),
            pl.BlockSpec(emat.shape, lambda b, c: (0, 0)),
            pl.BlockSpec((None,) + sshape, lambda b, c: (b, 0, 0, 0)),
        ],
        out_specs=out_specs,
        out_shape=out_shape,
        scratch_shapes=[pltpu.VMEM(sshape, F32)],
        compiler_params=_cp(("arbitrary", "arbitrary")),
        name=("ssd_lat" if need_y else "ssd_ctx") + ("_bwd" if reverse else "_fwd"),
    )(xbc, dt, arow, emat, s0)


def _fin1_kernel(yf_ref, yb_ref, z_ref, xs_ref, x_ref, g1_ref, dx_ref, nw_ref, wout_ref, o_ref):
    y = yf_ref[...] + yb_ref[...] + dx_ref[...] * xs_ref[...].astype(F32)
    y = y * _silu(z_ref[...].astype(F32))
    ms = jnp.mean(y * y, axis=-1, keepdims=True)
    yn = (y * lax.rsqrt(ms + EPS)) * nw_ref[...]
    m = _dot(yn.astype(BF16), wout_ref[...])
    x = jnp.concatenate([x_ref[k] for k in range(x_ref.shape[0])], axis=0)
    o_ref[...] = x + g1_ref[...] * m


def _fin1(yf, yb, z, xbc, xcol, g1, dx, nw, wout, n_lat):
    b_, _, _, d = xcol.shape
    n = n_lat
    rows = n_lat // GRID_W
    tt = TOK_TILE
    const = lambda b, q: (0, 0)
    tok = lambda w: pl.BlockSpec((None, tt, w), lambda b, q: (b, q, 0))
    return pl.pallas_call(
        _fin1_kernel,
        grid=(b_, n // tt),
        in_specs=[
            tok(M2_INNER), tok(M2_INNER), tok(M2_INNER), tok(M2_INNER),
            pl.BlockSpec((None, tt // rows, rows, d), lambda b, q: (b, q, 0, 0)),
            pl.BlockSpec((None, 1, d), lambda b, q: (b, 0, 0)),
            pl.BlockSpec((1, M2_INNER), const),
            pl.BlockSpec((1, M2_INNER), const),
            pl.BlockSpec(wout.shape, const),
        ],
        out_specs=tok(d),
        out_shape=jax.ShapeDtypeStruct((b_, n, d), F32),
        compiler_params=_cp(("parallel", "parallel")),
        name="fin1",
    )(yf, yb, z, xbc, xcol, g1, dx, nw, wout)


def kernel(x, c, ctx, c_ctx, w_mod, b_mod, norm_mix, norm_ffn, norm_final, router_w, router_b, exp_w1, exp_w3, exp_w2, ab_w_in, ab_w_out, s5_lam_re, s5_lam_im, s5_log_dt, s5_b_re, s5_b_im, s5_c_re, s5_c_im, s5_d, s5_w_glu, rg_conv_w, rg_conv_b, rg_wa, rg_ba, rg_wx, rg_bx, rg_lam, m2_w_in, m2_conv_w, m2_conv_b, m2_dt_bias, m2_a_log, m2_d, m2_norm, m2_w_out):
    b_, n_lat, d = x.shape
    n_ctx = ctx.shape[1]
    assert d == D_MODEL and b_ <= 7
    assert n_lat % (GRID_W * HALO) == 0 and n_lat // GRID_W == M2_CHUNK
    assert n_lat % TOK_TILE == 0 and n_ctx % TOK_TILE == 0 and n_ctx % GRID_W == 0
    nt = n_lat + n_ctx

    xall = jnp.concatenate([x, ctx], axis=1)
    c8 = jnp.concatenate([c, c_ctx[None], jnp.zeros((7 - b_, d), F32)], axis=0)
    mods = _modulation(c8, w_mod, b_mod)
    nrow = b_ + 1

    def mod_rows(layer):
        m = mods[layer, :nrow].reshape(nrow, 6, 1, d)
        sh1, sc1, g1, sh2, sc2, g2 = (m[:, k] for k in range(6))
        return (norm_mix[layer] * (1.0 + sc1), sh1, g1, norm_ffn[layer] * (1.0 + sc2), sh2, g2)

    rwh, rwl = _split2(jnp.pad(router_w.astype(F32), ((0, 0), (0, 128 - N_EXPERTS))))
    rb = jnp.pad(router_b.astype(F32), (0, 128 - N_EXPERTS)).reshape(1, 128)
    one_row = jnp.ones((1, d), F32)

    sc1, sh1, g1, sc2, sh2, g2 = mod_rows(0)
    cw8 = jnp.pad(rg_conv_w[0].astype(F32), ((0, 8 - CONV_K), (0, 0)))
    u, xrow, xc, gr = _inproj0b(xall, sc1, sh1, ab_w_in[0].astype(BF16), cw8,
                                rg_conv_b[0].astype(F32).reshape(1, RG_WIDTH), n_lat)
    tmat, wout5, win5, aq = _s5_chunk_params(s5_lam_re[0], s5_lam_im[0], s5_log_dt[0], s5_b_re[0], s5_b_im[0],
                                             s5_c_re[0], s5_c_im[0], b_)
    wa, wx, ba, bx, sp = _rg_params_b(rg_wa[0], rg_ba[0], rg_wx[0], rg_bx[0], rg_lam[0])
    yx, sfr, sfi, sbr, sbi = _s5_local(xrow, tmat, wout5)
    hr, hi, gbr, gbi, hgf, hgb = _scan0b(sfr, sfi, sbr, sbi, xc, aq, wa, wx, ba, bx, sp, n_lat)
    yx = _s5_state_out(hr, hi, gbr, gbi, win5, yx)
    x1 = _merge0b(yx, hgf, hgb, u, gr, xall, g1, s5_d[0].astype(F32).reshape(1, S5_WIDTH),
                  s5_w_glu[0].astype(BF16), ab_w_out[0].astype(BF16), n_lat)
    ew1, ew3, ew2 = exp_w1.astype(BF16), exp_w3.astype(BF16), exp_w2.astype(BF16)
    x2 = _moe(x1, sc2, sh2, g2, rwh, rwl, rb, ew1, ew3, ew2, 0, one_row, n_lat, final=False)

    sc1, sh1, g1, sc2, sh2, g2 = mod_rows(1)
    w_in = m2_w_in[0]
    wz = w_in[:, :M2_INNER].astype(BF16)
    wxbc = w_in[:, M2_INNER:M2_INNER + M2_CONV_DIM].astype(BF16)
    wd = jnp.pad(w_in[:, M2_INNER + M2_CONV_DIM:], ((0, 0), (0, 128 - 2 * M2_HEADS))).astype(BF16)
    cw8 = jnp.pad(m2_conv_w[0].astype(F32), ((0, 8 - CONV_K), (0, 0)))
    cb = m2_conv_b[0].astype(F32).reshape(1, M2_CONV_DIM)
    db = jnp.pad(m2_dt_bias[0].astype(F32).reshape(2 * M2_HEADS), (0, 128 - 2 * M2_HEADS)).reshape(1, 128)
    grows = n_lat // GRID_W
    x2v = x2.reshape(b_, nt // GRID_W, GRID_W, d).transpose(0, 2, 1, 3)
    z_l, xbc_l, dt_l = _inproj1(x2, x2v, sc1, sh1, wz, wxbc, wd, cw8, cb, db, n_lat, latent=True)
    _, xbc_c, dt_c = _inproj1(x2, x2v, sc1, sh1, wz, wxbc, wd, cw8, cb, db, n_lat, latent=False)
    a_neg = -jnp.exp(m2_a_log[0].astype(F32))
    lanes = jnp.arange(128)
    heads = jnp.arange(M2_INNER) // M2_HEAD_DIM
    s0 = jnp.zeros((b_, M2_GROUPS, M2_STATE, M2_HPG * M2_HEAD_DIM), F32)
    ys = []
    for dirn in range(2):
        rev = dirn == 1
        arow = jnp.zeros((128,), F32).at[dirn * M2_HEADS:(dirn + 1) * M2_HEADS].set(a_neg[dirn]).reshape(1, 128)
        emat = (lanes[:, None] == heads[None, :] + dirn * M2_HEADS).astype(BF16)
        (st,) = _ssd(xbc_c, dt_c, arow, emat, s0, rev, need_y=False)
        y, _ = _ssd(xbc_l, dt_l, arow, emat, st, rev, need_y=True)
        ys.append(y)
    dx = jnp.repeat(m2_d[0].astype(F32), M2_HEAD_DIM).reshape(1, M2_INNER)
    x3 = _fin1(ys[0], ys[1], z_l, xbc_l, x2v, g1, dx,
               m2_norm[0].astype(F32).reshape(1, M2_INNER), m2_w_out[0].astype(BF16), n_lat)
    out = _moe(x3, sc2, sh2, g2, rwh, rwl, rb, ew1, ew3, ew2, 1,
               norm_final.astype(F32).reshape(1, d), n_lat, final=True)
    return out.reshape(b_, GRID_W, grows, d).transpose(0, 2, 1, 3).reshape(b_, n_lat, d)
```

```python
import functools
import math

import jax
import jax.numpy as jnp
from jax import lax
from jax.experimental import pallas as pl
from jax.experimental.pallas import tpu as pltpu

F32 = jnp.float32
BF16 = jnp.bfloat16
I32 = jnp.int32

D_MODEL = 1024
GRID_W = 64
EPS = 1e-6

S5_WIDTH = 512
S5_GROUP = 16
S5_GROUPS = 32
S5_STATE = 64
RG_WIDTH = 512
RG_HEADS = 8
RG_HEAD_DIM = 64
RG_C = 8.0
CONV_K = 4

M2_INNER = 2048
M2_HEAD_DIM = 64
M2_HEADS = 32
M2_GROUPS = 8
M2_HPG = 4
M2_STATE = 128
M2_CHUNK = 128
M2_CONV_DIM = M2_INNER + 2 * M2_GROUPS * M2_STATE

N_EXPERTS = 16
N_EXPERT_GROUPS = 4
EXPERTS_PER_GROUP = 4
N_CLASSES = 24
MOE_BLOCK = 256
ROW_W = D_MODEL + 128

TOK_TILE = 256
INPROJ1_COLS = 4
HALO = 8
VMEM_LIMIT = 52 * 1024 * 1024


def _cp(sem, vmem=VMEM_LIMIT):
    return pltpu.CompilerParams(dimension_semantics=sem, vmem_limit_bytes=vmem)


def _dot(a, b):
    return jnp.dot(a, b, preferred_element_type=F32)


def _split2(a):
    hi = a.astype(BF16)
    lo = (a - hi.astype(F32)).astype(BF16)
    return hi, lo


def _split3(a):
    a1 = a.astype(BF16)
    r1 = a - a1.astype(F32)
    a2 = r1.astype(BF16)
    a3 = (r1 - a2.astype(F32)).astype(BF16)
    return a1, a2, a3


def _dot3(a, b):
    ah, al = _split2(a)
    bh, bl = _split2(b)
    return _dot(ah, bh) + (_dot(al, bh) + _dot(ah, bl))


def _norm_mod(x, scale, shift):
    ms = jnp.mean(x * x, axis=-1, keepdims=True)
    return (x * lax.rsqrt(ms + EPS)) * scale + shift


def _silu(x):
    return x * jax.nn.sigmoid(x)


def _gelu(x):
    c = math.sqrt(2.0 / math.pi)
    return 0.5 * x * (1.0 + jnp.tanh(c * (x + 0.044715 * (x * x * x))))


def _softplus(x):
    return jnp.maximum(x, 0.0) + jnp.log(1.0 + jnp.exp(-jnp.abs(x)))


def _centred_taps(xe, cw_ref, sl, tt):
    n = xe.shape[0]
    acc = cw_ref[1:2, sl] * xe[HALO:HALO + tt]
    for k in (0, 2, 3):
        shifted = pltpu.roll(xe, (n - (k - 1)) % n, axis=0)
        acc = acc + cw_ref[k:k + 1, sl] * shifted[HALO:HALO + tt]
    return acc


def _mod_kernel(c_ref, w_ref, b_ref, o_ref):
    c = c_ref[...]
    o_ref[...] = _dot3(_silu(c), w_ref[...]) + b_ref[...]


def _modulation(c8, w_mod, b_mod):
    depth, d, n = w_mod.shape
    tn = 1536
    return pl.pallas_call(
        _mod_kernel,
        grid=(depth, n // tn),
        in_specs=[pl.BlockSpec((8, d), lambda l, j: (0, 0)),
                  pl.BlockSpec((None, d, tn), lambda l, j: (l, 0, j)),
                  pl.BlockSpec((None, 1, tn), lambda l, j: (l, 0, j))],
        out_specs=pl.BlockSpec((None, 8, tn), lambda l, j: (l, 0, j)),
        out_shape=jax.ShapeDtypeStruct((depth, 8, n), F32),
        compiler_params=_cp(("parallel", "parallel")),
        name="modulation",
    )(c8, w_mod, b_mod.reshape(depth, 1, n))


S5_Q = 16
S5_PAIRS = S5_GROUPS // 2
PAIR_W = 2 * S5_Q * S5_GROUP
LB = 128
NLB = S5_WIDTH // LB


def _inproj0b_kernel(xp_ref, xc_ref, xn_ref, sc_ref, sh_ref, w_ref, cw_ref, cb_ref,
                     u_ref, xrow_ref, xcf_ref, gr_ref, us_s, *, nlt):
    i = pl.program_id(1)
    tt = TOK_TILE
    nch = tt // S5_Q
    has_prev = jnp.logical_and(i > 0, i < nlt).astype(F32)
    has_next = (i < nlt - 1).astype(F32)
    xe = jnp.concatenate([xp_ref[...], xc_ref[...], xn_ref[...]], axis=0)
    h = _norm_mod(xe, sc_ref[...], sh_ref[...]).astype(BF16)
    p = _dot(h, w_ref[...])
    u = p[HALO:HALO + tt, 0:S5_WIDTH]
    row = lax.broadcasted_iota(I32, (tt + 2 * HALO, 1), 0)
    keep = jnp.where(row < HALO, has_prev, jnp.where(row >= tt + HALO, has_next, 1.0))
    xr = p[:, S5_WIDTH:S5_WIDTH + RG_WIDTH] * keep
    xc = cb_ref[...] + _centred_taps(xr, cw_ref, slice(None), tt)
    u_ref[...] = u
    gr_ref[...] = p[HALO:HALO + tt, S5_WIDTH + RG_WIDTH:]
    xcf_ref[...] = xc
    for q in range(NLB):
        us_s[q] = u[:, LB * q:LB * (q + 1)]
    gpb = LB // S5_GROUP
    for q in range(NLB):
        steps = [us_s[q, pl.ds(s, nch, stride=S5_Q), :] for s in range(S5_Q)]
        for m in range(gpb // 2):
            halves = [jnp.concatenate([v[:, S5_GROUP * (2 * m + gl):S5_GROUP * (2 * m + gl + 1)] for v in steps],
                                      axis=1) for gl in range(2)]
            xrow_ref[(gpb // 2) * q + m] = jnp.concatenate(halves, axis=1).astype(BF16)


def _inproj0b(xall, sc1, sh1, w_in, cw8, cb, n_lat):
    b_, nt, d = xall.shape
    tt = TOK_TILE
    nch = tt // S5_Q
    nlt = n_lat // tt
    nti = nt // tt
    hb = tt // HALO
    nh = nt // HALO
    row_idx = lambda b, i: (jnp.where(i < nlt, b, b_), 0, 0)
    tok = lambda w: pl.BlockSpec((None, tt, w), lambda b, i: (b, i, 0))
    kern = functools.partial(_inproj0b_kernel, nlt=nlt)
    return pl.pallas_call(
        kern,
        grid=(b_, nti),
        in_specs=[
            pl.BlockSpec((None, HALO, d), lambda b, i: (b, jnp.maximum(i * hb - 1, 0), 0)),
            pl.BlockSpec((None, tt, d), lambda b, i: (b, i, 0)),
            pl.BlockSpec((None, HALO, d), lambda b, i: (b, jnp.minimum(i * hb + hb, nh - 1), 0)),
            pl.BlockSpec((None, 1, d), row_idx),
            pl.BlockSpec((None, 1, d), row_idx),
            pl.BlockSpec(w_in.shape, lambda b, i: (0, 0)),
            pl.BlockSpec((8, RG_WIDTH), lambda b, i: (0, 0)),
            pl.BlockSpec((1, RG_WIDTH), lambda b, i: (0, 0)),
        ],
        out_specs=[
            tok(S5_WIDTH),
            pl.BlockSpec((S5_PAIRS, nch, PAIR_W), lambda b, i: (0, i * b_ + b, 0)),
            tok(RG_WIDTH),
            tok(RG_WIDTH),
        ],
        out_shape=[
            jax.ShapeDtypeStruct((b_, nt, S5_WIDTH), F32),
            jax.ShapeDtypeStruct((S5_PAIRS, b_ * nt // S5_Q, PAIR_W), BF16),
            jax.ShapeDtypeStruct((b_, nt, RG_WIDTH), F32),
            jax.ShapeDtypeStruct((b_, nt, RG_WIDTH), F32),
        ],
        scratch_shapes=[pltpu.VMEM((NLB, tt, LB), F32)],
        compiler_params=_cp(("parallel", "parallel")),
        name="inproj0",
    )(xall, xall, xall, sc1, sh1, w_in, cw8, cb)


def _s5_local_kernel(x_ref, t_ref, wo_ref, yx_ref, fr_ref, fi_ref, br_ref, bi_ref):
    x = x_ref[...]
    yx_ref[...] = _dot(x, t_ref[...])
    sx = _dot(x, wo_ref[...])
    fr_ref[...] = sx[:, 0:LB]
    fi_ref[...] = sx[:, LB:2 * LB]
    br_ref[...] = sx[:, 2 * LB:3 * LB]
    bi_ref[...] = sx[:, 3 * LB:4 * LB]


def _s5_local(xrow, tmat, wout):
    npair, r, _ = xrow.shape
    pm = lambda w: pl.BlockSpec((None, r, w), lambda p: (p, 0, 0))
    wm = pl.BlockSpec((None, PAIR_W, PAIR_W), lambda p: (p, 0, 0))
    st = jax.ShapeDtypeStruct((npair, r, LB), F32)
    return pl.pallas_call(
        _s5_local_kernel,
        grid=(npair,),
        in_specs=[pm(PAIR_W), wm, wm],
        out_specs=[pm(PAIR_W), pm(LB), pm(LB), pm(LB), pm(LB)],
        out_shape=[jax.ShapeDtypeStruct((npair, r, PAIR_W), F32), st, st, st, st],
        compiler_params=_cp(("parallel",)),
        name="s5_local",
    )(xrow, tmat, wout)


def _s5_state_out_kernel(hr_ref, hi_ref, gr_ref, gi_ref, win_ref, yx_ref, o_ref):
    hg = jnp.concatenate([hr_ref[...], hi_ref[...], gr_ref[...], gi_ref[...]], axis=1).astype(BF16)
    o_ref[...] = yx_ref[...] + _dot(hg, win_ref[...])


def _s5_state_out(hr, hi, gr, gi, win, yx):
    npair, r, _ = yx.shape
    pm = lambda w: pl.BlockSpec((None, r, w), lambda p: (p, 0, 0))
    return pl.pallas_call(
        _s5_state_out_kernel,
        grid=(npair,),
        in_specs=[pm(LB), pm(LB), pm(LB), pm(LB), pl.BlockSpec((None, PAIR_W, PAIR_W), lambda p: (p, 0, 0)),
                  pm(PAIR_W)],
        out_specs=pm(PAIR_W),
        out_shape=jax.ShapeDtypeStruct((npair, r, PAIR_W), F32),
        input_output_aliases={5: 0},
        compiler_params=_cp(("parallel",)),
        name="s5_state_out",
    )(hr, hi, gr, gi, win, yx)


def _scan0b_kernel(fr_ref, fi_ref, br_ref, bi_ref, xcf_ref, xcb_ref, aq_ref, wa_ref, wx_ref, ba_ref, bx_ref,
                   sp_ref, hr_ref, hi_ref, gr_ref, gi_ref, hgf_ref, hgb_ref, st_s, ga_s, gb_s, hg_s):
    j = pl.program_id(0)
    nb, tt, _ = xcf_ref.shape
    nch = tt // S5_Q

    @pl.when(j == 0)
    def _():
        st_s[...] = jnp.zeros_like(st_s)
        hg_s[...] = jnp.zeros_like(hg_s)

    for d, (sr_ref, si_ref, or_ref, oi_ref) in enumerate(((fr_ref, fi_ref, hr_ref, hi_ref),
                                                          (br_ref, bi_ref, gr_ref, gi_ref))):
        for k in range(nch):
            c = k if d == 0 else nch - 1 - k
            rows = pl.ds(c, nb, stride=nch)
            for p in range(S5_PAIRS):
                hr = st_s[d, 0, p]
                hi = st_s[d, 1, p]
                or_ref[p, rows, :] = hr
                oi_ref[p, rows, :] = hi
                ar = aq_ref[d, 0, p]
                ai = aq_ref[d, 1, p]
                st_s[d, 0, p] = ar * hr - ai * hi + sr_ref[p, rows, :]
                st_s[d, 1, p] = ar * hi + ai * hr + si_ref[p, rows, :]

    sub = jnp.bitwise_and(lax.broadcasted_iota(I32, (tt, RG_WIDTH), 0), 7)
    ntl = tt // 8
    for d, x_ref in enumerate((xcf_ref, xcb_ref)):
        for b in range(nb):
            x = x_ref[b]
            xb = x.astype(BF16)
            log_a = (-RG_C) * jax.nn.sigmoid(_dot(xb, wa_ref[d]) + ba_ref[d]) * sp_ref[d]
            a = jnp.exp(log_a)
            bv = jnp.sqrt(1.0 - a * a) * jax.nn.sigmoid(_dot(xb, wx_ref[d]) + bx_ref[d]) * x
            for s in (1, 2, 4):
                if d == 0:
                    ok = sub >= s
                    shift = s
                else:
                    ok = sub < 8 - s
                    shift = tt - s
                a_prev = jnp.where(ok, pltpu.roll(a, shift, axis=0), 1.0)
                b_prev = jnp.where(ok, pltpu.roll(bv, shift, axis=0), 0.0)
                bv = bv + a * b_prev
                a = a * a_prev
            ga_s[d, b] = a
            gb_s[d, b] = bv

    def tile_step(k, carry):
        out = []
        for d in range(2):
            r0 = pl.multiple_of((k if d == 0 else ntl - 1 - k) * 8, 8)
            for b in range(nb):
                h = gb_s[d, b, pl.ds(r0, 8), :] + ga_s[d, b, pl.ds(r0, 8), :] * carry[d * nb + b]
                gb_s[d, b, pl.ds(r0, 8), :] = h
                last = h[7:8, :] if d == 0 else h[0:1, :]
                out.append(jnp.broadcast_to(last, (8, RG_WIDTH)))
        return tuple(out)

    init = tuple(hg_s[d, b] for d in range(2) for b in range(nb))
    fin = lax.fori_loop(0, ntl, tile_step, init)
    for d in range(2):
        for b in range(nb):
            hg_s[d, b] = fin[d * nb + b]
    for d, o_ref in enumerate((hgf_ref, hgb_ref)):
        for b in range(nb):
            o_ref[b] = gb_s[d, b]


def _scan0b(sfr, sfi, sbr, sbi, xc, aq, wa, wx, ba, bx, sp, n_lat):
    b_, nt, _ = xc.shape
    tt = TOK_TILE
    nch = tt // S5_Q
    ntile = nt // tt
    nlt = n_lat // tt
    nct = ntile - nlt
    tf = lambda j: jnp.where(j < nct, nlt + j, j - nct)
    tb = lambda j: ntile - 1 - j
    rblk = b_ * nch
    sf = pl.BlockSpec((S5_PAIRS, rblk, LB), lambda j: (0, tf(j), 0))
    sb = pl.BlockSpec((S5_PAIRS, rblk, LB), lambda j: (0, tb(j), 0))
    xf = pl.BlockSpec((b_, tt, RG_WIDTH), lambda j: (0, tf(j), 0))
    xb = pl.BlockSpec((b_, tt, RG_WIDTH), lambda j: (0, tb(j), 0))
    full = lambda a: pl.BlockSpec(a.shape, lambda j: (0,) * a.ndim)
    st = jax.ShapeDtypeStruct(sfr.shape, F32)
    hg = jax.ShapeDtypeStruct(xc.shape, F32)
    return pl.pallas_call(
        _scan0b_kernel,
        grid=(ntile,),
        in_specs=[sf, sf, sb, sb, xf, xb, full(aq), full(wa), full(wx), full(ba), full(bx), full(sp)],
        out_specs=[sf, sf, sb, sb, xf, xb],
        out_shape=[st, st, st, st, hg, hg],
        scratch_shapes=[
            pltpu.VMEM((2, 2, S5_PAIRS, b_, LB), F32),
            pltpu.VMEM((2, b_, tt, RG_WIDTH), F32),
            pltpu.VMEM((2, b_, tt, RG_WIDTH), F32),
            pltpu.VMEM((2, b_, 8, RG_WIDTH), F32),
        ],
        compiler_params=_cp(("arbitrary",)),
        name="scan0",
    )(sfr, sfi, sbr, sbi, xc, xc, aq, wa, wx, ba, bx, sp)


def _merge0b_kernel(yx_ref, hf_ref, hb_ref, u_ref, gr_ref, x_ref, g1_ref, d_ref, wglu_ref, wout_ref, o_ref, ys_s):
    nch = yx_ref.shape[1]
    gpb = LB // S5_GROUP
    rows = [yx_ref[p] for p in range(S5_PAIRS)]
    for t in range(S5_Q):
        for q in range(NLB):
            pieces = []
            for m in range(gpb // 2):
                r = rows[(gpb // 2) * q + m]
                for gl in range(2):
                    o = gl * S5_Q * S5_GROUP + t * S5_GROUP
                    pieces.append(r[:, o:o + S5_GROUP])
            ys_s[q, pl.ds(t, nch, stride=S5_Q), :] = jnp.concatenate(pieces, axis=1)
    y = jnp.concatenate([ys_s[q] for q in range(NLB)], axis=1) + d_ref[...] * u_ref[...]
    y = _gelu(y)
    y = y * jax.nn.sigmoid(_dot(y.astype(BF16), wglu_ref[...]))
    g = (hf_ref[...] + hb_ref[...]) * _gelu(gr_ref[...])
    m = _dot(jnp.concatenate([y, g], axis=1).astype(BF16), wout_ref[...])
    o_ref[...] = x_ref[...] + g1_ref[...] * m


def _merge0b(yx, hgf, hgb, u, gr, xall, g1, s5d, wglu, wout, n_lat):
    b_, nt, d = xall.shape
    tt = TOK_TILE
    nch = tt // S5_Q
    nlt = n_lat // tt
    row_idx = lambda b, i: (jnp.where(i < nlt, b, b_), 0, 0)
    tok = lambda w: pl.BlockSpec((None, tt, w), lambda b, i: (b, i, 0))
    const = lambda b, i: (0, 0)
    return pl.pallas_call(
        _merge0b_kernel,
        grid=(b_, nt // tt),
        in_specs=[
            pl.BlockSpec((S5_PAIRS, nch, PAIR_W), lambda b, i: (0, i * b_ + b, 0)),
            tok(RG_WIDTH), tok(RG_WIDTH), tok(S5_WIDTH), tok(RG_WIDTH), tok(d),
            pl.BlockSpec((None, 1, d), row_idx),
            pl.BlockSpec((1, S5_WIDTH), const),
            pl.BlockSpec(wglu.shape, const),
            pl.BlockSpec(wout.shape, const),
        ],
        out_specs=tok(d),
        out_shape=jax.ShapeDtypeStruct((b_, nt, d), F32),
        scratch_shapes=[pltpu.VMEM((NLB, tt, LB), F32)],
        compiler_params=_cp(("parallel", "parallel")),
        name="merge0",
    )(yx, hgf, hgb, u, gr, xall, g1, s5d, wglu, wout)


def _s5_chunk_params(lam_re, lam_im, log_dt, b_re, b_im, c_re, c_im, nb):
    hp = lax.Precision.HIGHEST
    qn = S5_Q
    lam_re = lam_re.astype(F32)
    lam_im = lam_im.astype(F32)
    dt = jnp.exp(log_dt.astype(F32))[..., None]
    n = jnp.arange(qn + 1, dtype=F32)[:, None, None, None]
    mag = jnp.exp(n * (lam_re * dt)[None])
    ang = n * (lam_im * dt)[None]
    zr = mag * jnp.cos(ang)
    zi = mag * jnp.sin(ang)
    ar, ai = zr[1], zi[1]
    den = lam_re * lam_re + lam_im * lam_im
    fr = ((ar - 1.0) * lam_re + ai * lam_im) / den
    fi = (ai * lam_re - (ar - 1.0) * lam_im) / den
    bbr = fr[..., None] * b_re - fi[..., None] * b_im
    bbi = fr[..., None] * b_im + fi[..., None] * b_re
    cr = c_re.astype(F32)
    ci = c_im.astype(F32)
    wr = zr[:qn, ..., None] * bbr[None] - zi[:qn, ..., None] * bbi[None]
    wi = zr[:qn, ..., None] * bbi[None] + zi[:qn, ..., None] * bbr[None]
    kern = (jnp.einsum('dgkp,ndgpj->ndgkj', cr, wr, precision=hp)
            - jnp.einsum('dgkp,ndgpj->ndgkj', ci, wi, precision=hp))
    s = jnp.arange(qn)
    lag = s[None, :] - s[:, None]
    tf = jnp.where((lag >= 0)[:, :, None, None, None], kern[jnp.clip(lag, 0, qn - 1), 0], 0.0)
    tb = jnp.where((lag <= 0)[:, :, None, None, None], kern[jnp.clip(-lag, 0, qn - 1), 1], 0.0)
    tg = (tf + tb).transpose(2, 0, 4, 1, 3).reshape(S5_GROUPS, qn * S5_GROUP, qn * S5_GROUP)

    def pair_diag(m):
        g, r, c = m.shape
        m = m.reshape(g // 2, 2, r, c)
        z = jnp.zeros((g // 2, 2, r, 2, c), F32)
        z = z.at[:, 0, :, 0, :].set(m[:, 0]).at[:, 1, :, 1, :].set(m[:, 1])
        return z.reshape(g // 2, 2 * r, 2 * c)

    tmat = pair_diag(tg).astype(BF16)
    def out_part(w, d, flip):
        m = w[::-1, d] if flip else w[:, d]
        return m.transpose(1, 0, 3, 2).reshape(S5_GROUPS, qn * S5_GROUP, S5_STATE)
    parts = [out_part(wr, 0, True), out_part(wi, 0, True), out_part(wr, 1, False), out_part(wi, 1, False)]
    wout = jnp.concatenate([pair_diag(pp) for pp in parts], axis=2).astype(BF16)
    def in_parts(d, powers):
        zr_p = zr[powers, d]
        zi_p = zi[powers, d]
        m_re = (cr[d][None] * zr_p[:, :, None, :] - ci[d][None] * zi_p[:, :, None, :])
        m_im = -(cr[d][None] * zi_p[:, :, None, :] + ci[d][None] * zr_p[:, :, None, :])
        f = lambda m: m.transpose(1, 3, 0, 2).reshape(S5_GROUPS, S5_STATE, qn * S5_GROUP)
        return f(m_re), f(m_im)
    fre, fim = in_parts(0, jnp.arange(1, qn + 1))
    bre, bim = in_parts(1, qn - jnp.arange(qn))
    win = jnp.concatenate([pair_diag(pp) for pp in (fre, fim, bre, bim)], axis=1).astype(BF16)
    aq = jnp.stack([zr[qn], zi[qn]], axis=1).reshape(2, 2, S5_PAIRS, 1, LB)
    return tmat, wout, win, jnp.broadcast_to(aq, (2, 2, S5_PAIRS, nb, LB))


def _rg_params_b(wa, ba, wx, bx, lam):
    eye = jnp.eye(RG_HEADS, dtype=F32)
    dense = lambda w: jnp.einsum('dhij,hk->dhikj', w.astype(F32), eye).reshape(2, RG_WIDTH, RG_WIDTH).astype(BF16)
    row = lambda v: v.astype(F32).reshape(2, 1, RG_WIDTH)
    return dense(wa), dense(wx), row(ba), row(bx), row(jax.nn.softplus(-lam.astype(F32)))


def _route_kernel(x_ref, sc_ref, sh_ref, rwh_ref, rwl_ref, rb_ref, hb_ref, meta_ref, cnt_ref, carry_s):
    first = jnp.logical_and(pl.program_id(0) == 0, pl.program_id(1) == 0)

    @pl.when(first)
    def _():
        carry_s[...] = jnp.zeros_like(carry_s)

    tt = TOK_TILE
    h = _norm_mod(x_ref[...], sc_ref[...], sh_ref[...])
    hh, hl = _split2(h)
    logits = _dot(hh, rwh_ref[...]) + (_dot(hl, rwh_ref[...]) + _dot(hh, rwl_ref[...]))
    lane = lax.broadcasted_iota(I32, (tt, 128), 1)
    valid = lane < N_EXPERTS
    neg = -1e30
    lg = jnp.where(valid, logits, neg)
    mx = jnp.max(lg, axis=-1, keepdims=True)
    ex = jnp.where(valid, jnp.exp(lg - mx), 0.0)
    probs = ex / jnp.sum(ex, axis=-1, keepdims=True)
    sel = probs + rb_ref[...]
    grp = jnp.right_shift(lane, 2)
    best = jnp.zeros((tt, 1), I32)
    bestv = jnp.max(jnp.where(jnp.logical_and(valid, grp == 0), sel, neg), axis=-1, keepdims=True)
    for k in range(1, N_EXPERT_GROUPS):
        gk = jnp.max(jnp.where(jnp.logical_and(valid, grp == k), sel, neg), axis=-1, keepdims=True)
        upd = gk > bestv
        best = jnp.where(upd, k, best)
        bestv = jnp.where(upd, gk, bestv)
    msel = jnp.where(jnp.logical_and(valid, grp == best), sel, neg)
    v1 = jnp.max(msel, axis=-1, keepdims=True)
    i1 = jnp.min(jnp.where(msel == v1, lane, 128), axis=-1, keepdims=True)
    msel2 = jnp.where(lane == i1, neg, msel)
    v2 = jnp.max(msel2, axis=-1, keepdims=True)
    i2 = jnp.min(jnp.where(msel2 == v2, lane, 128), axis=-1, keepdims=True)
    p1 = jnp.sum(jnp.where(lane == i1, probs, 0.0), axis=-1, keepdims=True)
    p2 = jnp.sum(jnp.where(lane == i2, probs, 0.0), axis=-1, keepdims=True)
    den = p1 + p2
    g1 = p1 / den
    g2 = p2 / den
    lo = jnp.minimum(i1, i2) - EXPERTS_PER_GROUP * best
    hi = jnp.maximum(i1, i2) - EXPERTS_PER_GROUP * best
    g_lo = jnp.where(i1 < i2, g1, g2)
    g_hi = jnp.where(i1 < i2, g2, g1)
    pair = jnp.right_shift(lo * (7 - lo), 1) + (hi - lo - 1)
    cls = best * 6 + pair

    onehot = jnp.where(lane == cls, 1.0, 0.0)
    r_i = lax.broadcasted_iota(I32, (tt, tt), 0)
    c_i = lax.broadcasted_iota(I32, (tt, tt), 1)
    before = jnp.where(c_i < r_i, 1.0, 0.0).astype(BF16)
    cum = _dot(before, onehot.astype(BF16))
    rank = jnp.sum(onehot * (cum + carry_s[...]), axis=-1, keepdims=True)
    carry_s[...] = carry_s[...] + jnp.sum(onehot, axis=0, keepdims=True)

    hb_ref[:, 0:D_MODEL] = h
    hb_ref[:, D_MODEL:] = jnp.where(lane == 0, g_lo, jnp.where(lane == 1, g_hi, 0.0))
    meta = jnp.where(lane == 0, cls.astype(F32), jnp.where(lane == 1, rank, 0.0))
    meta_ref[...] = meta.T[0:8, :].astype(I32)
    cnt_ref[...] = jnp.broadcast_to(carry_s[...], (8, 128))


def _route(x, sc2, sh2, rwh, rwl, rb, n_lat):
    b_, nt, d = x.shape
    tt = TOK_TILE
    nlt = n_lat // tt
    nti = nt // tt
    row_idx = lambda b, i: (jnp.where(i < nlt, b, b_), 0, 0)
    flat = lambda b, i: (b * nti + i, 0)
    return pl.pallas_call(
        _route_kernel,
        grid=(b_, nti),
        in_specs=[
            pl.BlockSpec((None, tt, d), lambda b, i: (b, i, 0)),
            pl.BlockSpec((None, 1, d), row_idx),
            pl.BlockSpec((None, 1, d), row_idx),
            pl.BlockSpec((d, 128), lambda b, i: (0, 0)),
            pl.BlockSpec((d, 128), lambda b, i: (0, 0)),
            pl.BlockSpec((1, 128), lambda b, i: (0, 0)),
        ],
        out_specs=[
            pl.BlockSpec((tt, ROW_W), flat),
            pl.BlockSpec((None, 8, tt), lambda b, i: (b * nti + i, 0, 0)),
            pl.BlockSpec((8, 128), lambda b, i: (0, 0)),
        ],
        out_shape=[
            jax.ShapeDtypeStruct((b_ * nt, ROW_W), F32),
            jax.ShapeDtypeStruct((b_ * nti, 8, tt), I32),
            jax.ShapeDtypeStruct((8, 128), F32),
        ],
        scratch_shapes=[pltpu.VMEM((1, 128), F32)],
        compiler_params=_cp(("arbitrary", "arbitrary")),
        name="route",
    )(x, sc2, sh2, rwh, rwl, rb)


def _row_copy(src, dst, src_row, dst_row, sem):
    return pltpu.make_async_copy(src.at[pl.ds(src_row, 1)], dst.at[pl.ds(dst_row, 1)], sem)


def _dest_row(ps_ref, meta_ref, r):
    return ps_ref[meta_ref[0, r]] + meta_ref[1, r]


ISSUE_UNROLL = 8


def _issue_rows(n, start_row):
    def body(j, carry):
        for k in range(ISSUE_UNROLL):
            start_row(j * ISSUE_UNROLL + k, k % 2)
        return carry

    lax.fori_loop(0, n // ISSUE_UNROLL, body, 0)


def _dispatch_kernel(ps_ref, meta_ref, hb_ref, xb_in, xb_out, ring, fsems, sems):
    del xb_in
    tt = TOK_TILE
    i = pl.program_id(0)
    n = pl.num_programs(0)
    slot = lax.rem(i, 2)
    rslot = lax.rem(i, 3)

    def fetch(t, s):
        return pltpu.make_async_copy(hb_ref.at[pl.ds(t * tt, tt)], ring.at[s], fsems.at[s])

    @pl.when(i == 0)
    def _():
        fetch(0, 0).start()

    @pl.when(i + 1 < n)
    def _():
        fetch(i + 1, lax.rem(i + 1, 3)).start()

    fetch(i, rslot).wait()
    src = ring.at[rslot]

    def start_row(r, prio):
        _row_copy(src, xb_out, r, _dest_row(ps_ref, meta_ref, r), sems.at[slot]).start(priority=prio)

    _issue_rows(tt, start_row)

    def drain(s):
        pltpu.make_async_copy(src, xb_out.at[pl.ds(0, tt)], sems.at[s]).wait()

    @pl.when(i > 0)
    def _():
        drain(1 - slot)

    @pl.when(i == n - 1)
    def _():
        drain(slot)


def _dispatch(pstart, meta, hb, n_rows):
    t = hb.shape[0]
    tt = TOK_TILE
    zeros = jnp.zeros((n_rows, ROW_W), F32)
    gs = pltpu.PrefetchScalarGridSpec(
        num_scalar_prefetch=1,
        grid=(t // tt,),
        in_specs=[
            pl.BlockSpec((None, 8, tt), lambda i, ps: (i, 0, 0), memory_space=pltpu.SMEM),
            pl.BlockSpec(memory_space=pl.ANY),
            pl.BlockSpec(memory_space=pl.ANY),
        ],
        out_specs=pl.BlockSpec(memory_space=pl.ANY),
        scratch_shapes=[pltpu.VMEM((3, tt, ROW_W), F32), pltpu.SemaphoreType.DMA((3,)),
                        pltpu.SemaphoreType.DMA((2,))],
    )
    return pl.pallas_call(
        _dispatch_kernel,
        grid_spec=gs,
        out_shape=jax.ShapeDtypeStruct((n_rows, ROW_W), F32),
        input_output_aliases={3: 0},
        compiler_params=_cp(("arbitrary",)),
        name="dispatch",
    )(pstart, meta, hb, zeros)


def _expert_kernel(ea_ref, eb_ref, nu_ref, x_ref, w1a, w1b, w3a, w3b, w2a, w2b, o_ref):
    del ea_ref, eb_ref
    j = pl.program_id(0)

    @pl.when(j < nu_ref[0])
    def _():
        x = x_ref[:, 0:D_MODEL].astype(BF16)
        g_lo = x_ref[:, D_MODEL:D_MODEL + 1]
        g_hi = x_ref[:, D_MODEL + 1:D_MODEL + 2]

        def ffn(w1, w3, w2):
            a = _dot(x, w1[...])
            b = _dot(x, w3[...])
            return _dot((_silu(a) * b).astype(BF16), w2[...])

        o_ref[...] = g_lo * ffn(w1a, w3a, w2a) + g_hi * ffn(w1b, w3b, w2b)

    @pl.when(j >= nu_ref[0])
    def _():
        o_ref[...] = jnp.zeros_like(o_ref)


def _experts(blk_ea, blk_eb, nused, xb, w1, w3, w2, layer):
    n_rows = xb.shape[0]
    nb = n_rows // MOE_BLOCK
    d = D_MODEL
    wa = lambda j, ea, eb, nu: (layer, ea[j], 0, 0)
    wb = lambda j, ea, eb, nu: (layer, eb[j], 0, 0)
    wspec = lambda f: pl.BlockSpec((None, None, d, d), f)
    gs = pltpu.PrefetchScalarGridSpec(
        num_scalar_prefetch=3,
        grid=(nb,),
        in_specs=[pl.BlockSpec((MOE_BLOCK, ROW_W), lambda j, ea, eb, nu: (j, 0)),
                  wspec(wa), wspec(wb), wspec(wa), wspec(wb), wspec(wa), wspec(wb)],
        out_specs=pl.BlockSpec((MOE_BLOCK, d), lambda j, ea, eb, nu: (j, 0)),
    )
    return pl.pallas_call(
        _expert_kernel,
        grid_spec=gs,
        out_shape=jax.ShapeDtypeStruct((n_rows, d), F32),
        compiler_params=_cp(("arbitrary",)),
        name="experts",
    )(blk_ea, blk_eb, nused, xb, w1, w1, w3, w3, w2, w2)


def _combine_kernel(ps_ref, meta_ref, metan_ref, yb_ref, x_ref, g2_ref, nw_ref, o_ref, buf, sems, *, final):
    tt = TOK_TILE
    i = pl.program_id(0)
    n = pl.num_programs(0)
    slot = lax.rem(i, 2)

    def gather(mref, s):
        def start_row(r, prio):
            _row_copy(yb_ref, buf.at[s], _dest_row(ps_ref, mref, r), r, sems.at[s]).start(priority=prio)

        _issue_rows(tt, start_row)

    @pl.when(i == 0)
    def _():
        gather(meta_ref, 0)

    @pl.when(i + 1 < n)
    def _():
        gather(metan_ref, 1 - slot)

    pltpu.make_async_copy(yb_ref.at[pl.ds(0, tt)], buf.at[slot], sems.at[slot]).wait()
    x2 = x_ref[...] + g2_ref[...] * buf[slot]
    if final:
        ms = jnp.mean(x2 * x2, axis=-1, keepdims=True)
        x2 = x2 * lax.rsqrt(ms + EPS) * nw_ref[...]
    o_ref[...] = x2


def _combine(pstart, meta, yb, x, g2, nw, n_lat, final):
    b_, nt, d = x.shape
    tt = TOK_TILE
    nlt = n_lat // tt
    nti = nt // tt
    ntile = b_ * nti
    row_idx = lambda i, ps: (jnp.where(lax.rem(i, nti) < nlt, i // nti, b_), 0, 0)
    kern = functools.partial(_combine_kernel, final=final)
    gs = pltpu.PrefetchScalarGridSpec(
        num_scalar_prefetch=1,
        grid=(ntile,),
        in_specs=[
            pl.BlockSpec((None, 8, tt), lambda i, ps: (i, 0, 0), memory_space=pltpu.SMEM),
            pl.BlockSpec((None, 8, tt), lambda i, ps: (jnp.minimum(i + 1, ntile - 1), 0, 0),
                         memory_space=pltpu.SMEM),
            pl.BlockSpec(memory_space=pl.ANY),
            pl.BlockSpec((tt, d), lambda i, ps: (i, 0)),
            pl.BlockSpec((None, 1, d), row_idx),
            pl.BlockSpec((1, d), lambda i, ps: (0, 0)),
        ],
        out_specs=pl.BlockSpec((tt, d), lambda i, ps: (i, 0)),
        scratch_shapes=[pltpu.VMEM((2, tt, d), F32), pltpu.SemaphoreType.DMA((2,))],
    )
    out = pl.pallas_call(
        kern,
        grid_spec=gs,
        out_shape=jax.ShapeDtypeStruct((b_ * nt, d), F32),
        compiler_params=_cp(("arbitrary",)),
        name="combine",
    )(pstart, meta, meta, yb, x.reshape(b_ * nt, d), g2, nw)
    return out.reshape(b_, nt, d)


def _class_tables():
    ea, eb = [], []
    for g in range(N_EXPERT_GROUPS):
        for lo in range(EXPERTS_PER_GROUP):
            for hi in range(lo + 1, EXPERTS_PER_GROUP):
                ea.append(g * EXPERTS_PER_GROUP + lo)
                eb.append(g * EXPERTS_PER_GROUP + hi)
    return jnp.array(ea, I32), jnp.array(eb, I32)


def _moe(x, sc2, sh2, g2, rwh, rwl, rb, w1, w3, w2, layer, nw, n_lat, final):
    b_, nt, d = x.shape
    t = b_ * nt
    hb, meta, cnt = _route(x, sc2, sh2, rwh, rwl, rb, n_lat)
    counts = cnt[0, :N_CLASSES].astype(I32)
    padded = (counts + MOE_BLOCK - 1) // MOE_BLOCK * MOE_BLOCK
    pend = jnp.cumsum(padded)
    pstart = pend - padded
    nb = t // MOE_BLOCK + N_CLASSES
    blk_start = jnp.arange(nb, dtype=I32)[:, None] * MOE_BLOCK
    blk_cls = jnp.minimum(jnp.sum((pend[None, :] <= blk_start).astype(I32), axis=1), N_CLASSES - 1)
    ea, eb = _class_tables()
    nused = (pend[-1] // MOE_BLOCK).reshape(1).astype(I32)
    xb = _dispatch(pstart, meta, hb, nb * MOE_BLOCK)
    yb = _experts(jnp.take(ea, blk_cls), jnp.take(eb, blk_cls), nused, xb, w1, w3, w2, layer)
    return _combine(pstart, meta, yb, x, g2, nw, n_lat, final)


def _inproj1_kernel(*refs, latent, nq):
    if latent:
        (xp_ref, xm_ref, xn_ref, sc_ref, sh_ref, wz_ref, wx_ref, wd_ref, cw_ref, cb_ref, db_ref,
         z_ref, xbc_ref, dt_ref) = refs
        q = pl.program_id(1)
        has_prev = (q > 0).astype(F32)
        has_next = (q < nq - 1).astype(F32)
        cols = [xm_ref[k] for k in range(xm_ref.shape[0])]
        xe = jnp.concatenate([xp_ref[...]] + cols + [xn_ref[...]], axis=0)
    else:
        (xm_ref, sc_ref, sh_ref, wz_ref, wx_ref, wd_ref, cw_ref, cb_ref, db_ref,
         z_ref, xbc_ref, dt_ref) = refs
        has_prev = 0.0
        has_next = 0.0
        pad = jnp.zeros((HALO, D_MODEL), F32)
        xe = jnp.concatenate([pad, xm_ref[...], pad], axis=0)
    tt = z_ref.shape[0]
    h = _norm_mod(xe, sc_ref[...], sh_ref[...]).astype(BF16)
    hc = h[HALO:HALO + tt]
    z_ref[...] = _dot(hc, wz_ref[...]).astype(BF16)
    row = lax.broadcasted_iota(I32, (tt + 2 * HALO, 1), 0)
    keep = jnp.where(row < HALO, has_prev, jnp.where(row >= tt + HALO, has_next, 1.0))
    pw = 1024
    for pc in range(M2_CONV_DIM // pw):
        sl = slice(pc * pw, (pc + 1) * pw)
        pr = _dot(h, wx_ref[:, sl]) * keep
        acc = cb_ref[:, sl] + _centred_taps(pr, cw_ref, sl, tt)
        xbc_ref[:, sl] = _silu(acc).astype(BF16)
    lane = lax.broadcasted_iota(I32, (tt, 128), 1)
    dt = _softplus(_dot(hc, wd_ref[...]) + db_ref[...])
    dt_ref[...] = jnp.where(lane < 2 * M2_HEADS, dt, 0.0)


def _inproj1(xall, xcol, sc1, sh1, wz, wx, wd, cw8, cb, db, n_lat, latent):
    b_, nt, d = xall.shape
    if latent:
        n = n_lat
        rows = n_lat // GRID_W
        nc = INPROJ1_COLS
        tt = nc * rows
        nq = GRID_W // nc
        rb = rows // HALO
        x_specs = [
            pl.BlockSpec((None, None, HALO, d), lambda b, q: (b, jnp.maximum(nc * q - 1, 0), rb - 1, 0)),
            pl.BlockSpec((None, nc, rows, d), lambda b, q: (b, q, 0, 0)),
            pl.BlockSpec((None, None, HALO, d), lambda b, q: (b, jnp.minimum(nc * q + nc, GRID_W - 1), 0, 0)),
        ]
        x_args = [xcol, xcol, xcol]
        row_idx = lambda b, q: (b, 0, 0)
    else:
        tt = TOK_TILE
        n = nt - n_lat
        nq = n // tt
        off = n_lat // tt
        x_specs = [pl.BlockSpec((None, tt, d), lambda b, q: (b, off + q, 0))]
        x_args = [xall]
        row_idx = lambda b, q: (b_, 0, 0)
    const = lambda b, q: (0, 0)
    kern = functools.partial(_inproj1_kernel, latent=latent, nq=nq)
    return pl.pallas_call(
        kern,
        grid=(b_, nq),
        in_specs=x_specs + [
            pl.BlockSpec((None, 1, d), row_idx),
            pl.BlockSpec((None, 1, d), row_idx),
            pl.BlockSpec(wz.shape, const, pipeline_mode=pl.Buffered(1)),
            pl.BlockSpec(wx.shape, const, pipeline_mode=pl.Buffered(1)),
            pl.BlockSpec(wd.shape, const, pipeline_mode=pl.Buffered(1)),
            pl.BlockSpec(cw8.shape, const),
            pl.BlockSpec(cb.shape, const),
            pl.BlockSpec(db.shape, const),
        ],
        out_specs=[
            pl.BlockSpec((None, tt, M2_INNER), lambda b, q: (b, q, 0)),
            pl.BlockSpec((None, tt, M2_CONV_DIM), lambda b, q: (b, q, 0)),
            pl.BlockSpec((None, tt, 128), lambda b, q: (b, q, 0)),
        ],
        out_shape=[
            jax.ShapeDtypeStruct((b_, n, M2_INNER), BF16),
            jax.ShapeDtypeStruct((b_, n, M2_CONV_DIM), BF16),
            jax.ShapeDtypeStruct((b_, n, 128), F32),
        ],
        compiler_params=_cp(("parallel", "parallel")),
        name="inproj1_lat" if latent else "inproj1_ctx",
    )(*x_args, sc1, sh1, wz, wx, wd, cw8, cb, db)


def _ssd_kernel(*refs, reverse, need_y, lane0):
    if need_y:
        xbc_ref, dt_ref, arow_ref, e_ref, s0_ref, y_ref, sf_ref, s_s = refs
    else:
        xbc_ref, dt_ref, arow_ref, e_ref, s0_ref, sf_ref, s_s = refs
    c = pl.program_id(1)
    nc = pl.num_programs(1)
    q = M2_CHUNK

    @pl.when(c == 0)
    def _():
        s_s[...] = s0_ref[...]

    dt = dt_ref[...]
    a = dt * arow_ref[...]
    row = lax.broadcasted_iota(I32, (q, q), 0)
    col = lax.broadcasted_iota(I32, (q, q), 1)
    incl = (col >= row) if reverse else (col <= row)
    incl_t = (row >= col) if reverse else (row <= col)
    lt = jnp.where(incl, 1.0, 0.0).astype(BF16)
    lt_t = jnp.where(incl_t, 1.0, 0.0).astype(BF16)
    a1, a2, a3 = _split3(a)
    cum = _dot(lt, a1) + (_dot(lt, a2) + _dot(lt, a3))
    b1, b2, b3 = _split3(a.T)
    cum_t = _dot(b1, lt_t) + (_dot(b2, lt_t) + _dot(b3, lt_t))
    atot = cum[0:1, :] if reverse else cum[q - 1:q, :]
    wexp = _dot((dt * jnp.exp(atot - cum)).astype(BF16), e_ref[...])
    dt_t = dt.T
    eh, el = _split2(jnp.broadcast_to(jnp.exp(atot), (8, 128)))
    atx = _dot(eh, e_ref[...]) + _dot(el, e_ref[...])

    for g in range(M2_GROUPS):
        bg = xbc_ref[:, M2_INNER + M2_STATE * g:M2_INNER + M2_STATE * (g + 1)]
        gw = M2_HPG * M2_HEAD_DIM
        xg = xbc_ref[:, gw * g:gw * (g + 1)]
        sg = s_s[g]
        if need_y:
            cg = xbc_ref[:, M2_INNER + M2_GROUPS * M2_STATE + M2_STATE * g:
                         M2_INNER + M2_GROUPS * M2_STATE + M2_STATE * (g + 1)]
            cb = lax.dot_general(cg, bg, (((1,), (1,)), ((), ())), preferred_element_type=F32)
            cg32 = cg.astype(F32)
            sgb = sg.astype(BF16)
            y_parts = []
            for r in range(M2_HPG):
                hl = lane0 + M2_HPG * g + r
                ps = slice(M2_HEAD_DIM * r, M2_HEAD_DIM * (r + 1))
                colb = jnp.broadcast_to(cum[:, hl:hl + 1], (q, q))
                rowb = jnp.broadcast_to(cum_t[hl:hl + 1, :], (q, q))
                lmat = jnp.exp(jnp.where(incl, colb - rowb, -1e30))
                gm = (cb * lmat * dt_t[hl:hl + 1, :]).astype(BF16)
                cd = (jnp.exp(colb) * cg32).astype(BF16)
                lhs = jnp.concatenate([gm, cd], axis=1)
                rhs = jnp.concatenate([xg[:, ps], sgb[:, ps]], axis=0)
                y_parts.append(_dot(lhs, rhs))
        xw = (xg.astype(F32) * wexp[:, gw * g:gw * (g + 1)]).astype(BF16)
        snew = lax.dot_general(bg, xw, (((0,), (0,)), ((), ())), preferred_element_type=F32)
        s_s[g] = atx[0:1, gw * g:gw * (g + 1)] * sg + snew
        if need_y:
            y_ref[:, gw * g:gw * (g + 1)] = jnp.concatenate(y_parts, axis=1).astype(BF16)

    @pl.when(c == nc - 1)
    def _():
        sf_ref[...] = s_s[...]


def _ssd(xbc, dt, arow, emat, s0, reverse, need_y):
    b_, n, _ = xbc.shape
    q = M2_CHUNK
    nc = n // q
    cidx = (lambda c: nc - 1 - c) if reverse else (lambda c: c)
    sshape = (M2_GROUPS, M2_STATE, M2_HPG * M2_HEAD_DIM)
    kern = functools.partial(_ssd_kernel, reverse=reverse, need_y=need_y, lane0=M2_HEADS if reverse else 0)
    out_specs = [pl.BlockSpec((None,) + sshape, lambda b, c: (b, 0, 0, 0))]
    out_shape = [jax.ShapeDtypeStruct((b_,) + sshape, F32)]
    if need_y:
        out_specs = [pl.BlockSpec((None, q, M2_INNER), lambda b, c: (b, cidx(c), 0))] + out_specs
        out_shape = [jax.ShapeDtypeStruct((b_, n, M2_INNER), BF16)] + out_shape
    return pl.pallas_call(
        kern,
        grid=(b_, nc),
        in_specs=[
            pl.BlockSpec((None, q, M2_CONV_DIM), lambda b, c: (b, cidx(c), 0)),
            pl.BlockSpec((None, q, 128), lambda b, c: (b, cidx(c), 0)),
            pl.BlockSpec((1, 128), lambda b, c: (0, 0)),
            pl.BlockSpec(emat.shape, lambda b, c: (0, 0)),
            pl.BlockSpec((None,) + sshape, lambda b, c: (b, 0, 0, 0)),
        ],
        out_specs=out_specs,
        out_shape=out_shape,
        scratch_shapes=[pltpu.VMEM(sshape, F32)],
        compiler_params=_cp(("arbitrary", "arbitrary")),
        name=("ssd_lat" if need_y else "ssd_ctx") + ("_bwd" if reverse else "_fwd"),
    )(xbc, dt, arow, emat, s0)


def _fin1_kernel(yf_ref, yb_ref, z_ref, xs_ref, x_ref, g1_ref, dx_ref, nw_ref, wout_ref, o_ref):
    y = yf_ref[...].astype(F32) + yb_ref[...].astype(F32) + dx_ref[...] * xs_ref[...].astype(F32)
    y = y * _silu(z_ref[...].astype(F32))
    ms = jnp.mean(y * y, axis=-1, keepdims=True)
    yn = (y * lax.rsqrt(ms + EPS)) * nw_ref[...]
    m = _dot(yn.astype(BF16), wout_ref[...])
    x = jnp.concatenate([x_ref[k] for k in range(x_ref.shape[0])], axis=0)
    o_ref[...] = x + g1_ref[...] * m


def _fin1(yf, yb, z, xbc, xcol, g1, dx, nw, wout, n_lat):
    b_, _, _, d = xcol.shape
    n = n_lat
    rows = n_lat // GRID_W
    tt = TOK_TILE
    const = lambda b, q: (0, 0)
    tok = lambda w: pl.BlockSpec((None, tt, w), lambda b, q: (b, q, 0))
    return pl.pallas_call(
        _fin1_kernel,
        grid=(b_, n // tt),
        in_specs=[
            tok(M2_INNER), tok(M2_INNER), tok(M2_INNER), tok(M2_INNER),
            pl.BlockSpec((None, tt // rows, rows, d), lambda b, q: (b, q, 0, 0)),
            pl.BlockSpec((None, 1, d), lambda b, q: (b, 0, 0)),
            pl.BlockSpec((1, M2_INNER), const),
            pl.BlockSpec((1, M2_INNER), const),
            pl.BlockSpec(wout.shape, const),
        ],
        out_specs=tok(d),
        out_shape=jax.ShapeDtypeStruct((b_, n, d), F32),
        compiler_params=_cp(("parallel", "parallel")),
        name="fin1",
    )(yf, yb, z, xbc, xcol, g1, dx, nw, wout)


def kernel(x, c, ctx, c_ctx, w_mod, b_mod, norm_mix, norm_ffn, norm_final, router_w, router_b, exp_w1, exp_w3, exp_w2, ab_w_in, ab_w_out, s5_lam_re, s5_lam_im, s5_log_dt, s5_b_re, s5_b_im, s5_c_re, s5_c_im, s5_d, s5_w_glu, rg_conv_w, rg_conv_b, rg_wa, rg_ba, rg_wx, rg_bx, rg_lam, m2_w_in, m2_conv_w, m2_conv_b, m2_dt_bias, m2_a_log, m2_d, m2_norm, m2_w_out):
    b_, n_lat, d = x.shape
    n_ctx = ctx.shape[1]
    assert d == D_MODEL and b_ <= 7
    assert n_lat % (GRID_W * HALO) == 0 and n_lat // GRID_W == M2_CHUNK
    assert n_lat % TOK_TILE == 0 and n_ctx % TOK_TILE == 0 and n_ctx % GRID_W == 0
    nt = n_lat + n_ctx

    xall = jnp.concatenate([x, ctx], axis=1)
    c8 = jnp.concatenate([c, c_ctx[None], jnp.zeros((7 - b_, d), F32)], axis=0)
    mods = _modulation(c8, w_mod, b_mod)
    nrow = b_ + 1

    def mod_rows(layer):
        m = mods[layer, :nrow].reshape(nrow, 6, 1, d)
        sh1, sc1, g1, sh2, sc2, g2 = (m[:, k] for k in range(6))
        return (norm_mix[layer] * (1.0 + sc1), sh1, g1, norm_ffn[layer] * (1.0 + sc2), sh2, g2)

    rwh, rwl = _split2(jnp.pad(router_w.astype(F32), ((0, 0), (0, 128 - N_EXPERTS))))
    rb = jnp.pad(router_b.astype(F32), (0, 128 - N_EXPERTS)).reshape(1, 128)
    one_row = jnp.ones((1, d), F32)

    sc1, sh1, g1, sc2, sh2, g2 = mod_rows(0)
    cw8 = jnp.pad(rg_conv_w[0].astype(F32), ((0, 8 - CONV_K), (0, 0)))
    u, xrow, xc, gr = _inproj0b(xall, sc1, sh1, ab_w_in[0].astype(BF16), cw8,
                                rg_conv_b[0].astype(F32).reshape(1, RG_WIDTH), n_lat)
    tmat, wout5, win5, aq = _s5_chunk_params(s5_lam_re[0], s5_lam_im[0], s5_log_dt[0], s5_b_re[0], s5_b_im[0],
                                             s5_c_re[0], s5_c_im[0], b_)
    wa, wx, ba, bx, sp = _rg_params_b(rg_wa[0], rg_ba[0], rg_wx[0], rg_bx[0], rg_lam[0])
    yx, sfr, sfi, sbr, sbi = _s5_local(xrow, tmat, wout5)
    hr, hi, gbr, gbi, hgf, hgb = _scan0b(sfr, sfi, sbr, sbi, xc, aq, wa, wx, ba, bx, sp, n_lat)
    yx = _s5_state_out(hr, hi, gbr, gbi, win5, yx)
    x1 = _merge0b(yx, hgf, hgb, u, gr, xall, g1, s5_d[0].astype(F32).reshape(1, S5_WIDTH),
                  s5_w_glu[0].astype(BF16), ab_w_out[0].astype(BF16), n_lat)
    ew1, ew3, ew2 = exp_w1.astype(BF16), exp_w3.astype(BF16), exp_w2.astype(BF16)
    x2 = _moe(x1, sc2, sh2, g2, rwh, rwl, rb, ew1, ew3, ew2, 0, one_row, n_lat, final=False)

    sc1, sh1, g1, sc2, sh2, g2 = mod_rows(1)
    w_in = m2_w_in[0]
    wz = w_in[:, :M2_INNER].astype(BF16)
    wxbc = w_in[:, M2_INNER:M2_INNER + M2_CONV_DIM].astype(BF16)
    wd = jnp.pad(w_in[:, M2_INNER + M2_CONV_DIM:], ((0, 0), (0, 128 - 2 * M2_HEADS))).astype(BF16)
    cw8 = jnp.pad(m2_conv_w[0].astype(F32), ((0, 8 - CONV_K), (0, 0)))
    cb = m2_conv_b[0].astype(F32).reshape(1, M2_CONV_DIM)
    db = jnp.pad(m2_dt_bias[0].astype(F32).reshape(2 * M2_HEADS), (0, 128 - 2 * M2_HEADS)).reshape(1, 128)
    grows = n_lat // GRID_W
    x2v = x2.reshape(b_, nt // GRID_W, GRID_W, d).transpose(0, 2, 1, 3)
    z_l, xbc_l, dt_l = _inproj1(x2, x2v, sc1, sh1, wz, wxbc, wd, cw8, cb, db, n_lat, latent=True)
    _, xbc_c, dt_c = _inproj1(x2, x2v, sc1, sh1, wz, wxbc, wd, cw8, cb, db, n_lat, latent=False)
    a_neg = -jnp.exp(m2_a_log[0].astype(F32))
    lanes = jnp.arange(128)
    heads = jnp.arange(M2_INNER) // M2_HEAD_DIM
    s0 = jnp.zeros((b_, M2_GROUPS, M2_STATE, M2_HPG * M2_HEAD_DIM), F32)
    ys = []
    for dirn in range(2):
        rev = dirn == 1
        arow = jnp.zeros((128,), F32).at[dirn * M2_HEADS:(dirn + 1) * M2_HEADS].set(a_neg[dirn]).reshape(1, 128)
        emat = (lanes[:, None] == heads[None, :] + dirn * M2_HEADS).astype(BF16)
        (st,) = _ssd(xbc_c, dt_c, arow, emat, s0, rev, need_y=False)
        y, _ = _ssd(xbc_l, dt_l, arow, emat, st, rev, need_y=True)
        ys.append(y)
    dx = jnp.repeat(m2_d[0].astype(F32), M2_HEAD_DIM).reshape(1, M2_INNER)
    x3 = _fin1(ys[0], ys[1], z_l, xbc_l, x2v, g1, dx,
               m2_norm[0].astype(F32).reshape(1, M2_INNER), m2_w_out[0].astype(BF16), n_lat)
    out = _moe(x3, sc2, sh2, g2, rwh, rwl, rb, ew1, ew3, ew2, 1,
               norm_final.astype(F32).reshape(1, d), n_lat, final=True)
    return out.reshape(b_, GRID_W, grows, d).transpose(0, 2, 1, 3).reshape(b_, n_lat, d)
```

```python
import functools
import math

import jax
import jax.numpy as jnp
from jax import lax
from jax.experimental import pallas as pl
from jax.experimental.pallas import tpu as pltpu

F32 = jnp.float32
BF16 = jnp.bfloat16
I32 = jnp.int32

D_MODEL = 1024
GRID_W = 64
EPS = 1e-6

S5_WIDTH = 512
S5_GROUP = 16
S5_GROUPS = 32
S5_STATE = 64
RG_WIDTH = 512
RG_HEADS = 8
RG_HEAD_DIM = 64
RG_C = 8.0
CONV_K = 4

M2_INNER = 2048
M2_HEAD_DIM = 64
M2_HEADS = 32
M2_GROUPS = 8
M2_HPG = 4
M2_STATE = 128
M2_CHUNK = 128
M2_CONV_DIM = M2_INNER + 2 * M2_GROUPS * M2_STATE

N_EXPERTS = 16
N_EXPERT_GROUPS = 4
EXPERTS_PER_GROUP = 4
N_CLASSES = 24
MOE_BLOCK = 256
ROW_W = D_MODEL + 128

TOK_TILE = 256
INPROJ1_COLS = 4
HALO = 8
VMEM_LIMIT = 52 * 1024 * 1024


def _cp(sem, vmem=VMEM_LIMIT):
    return pltpu.CompilerParams(dimension_semantics=sem, vmem_limit_bytes=vmem)


def _dot(a, b):
    return jnp.dot(a, b, preferred_element_type=F32)


def _split2(a):
    hi = a.astype(BF16)
    lo = (a - hi.astype(F32)).astype(BF16)
    return hi, lo


def _split3(a):
    a1 = a.astype(BF16)
    r1 = a - a1.astype(F32)
    a2 = r1.astype(BF16)
    a3 = (r1 - a2.astype(F32)).astype(BF16)
    return a1, a2, a3


def _dot3(a, b):
    ah, al = _split2(a)
    bh, bl = _split2(b)
    return _dot(ah, bh) + (_dot(al, bh) + _dot(ah, bl))


def _norm_mod(x, scale, shift):
    ms = jnp.mean(x * x, axis=-1, keepdims=True)
    return (x * lax.rsqrt(ms + EPS)) * scale + shift


def _silu(x):
    return x * jax.nn.sigmoid(x)


def _gelu(x):
    c = math.sqrt(2.0 / math.pi)
    return 0.5 * x * (1.0 + jnp.tanh(c * (x + 0.044715 * (x * x * x))))


def _softplus(x):
    return jnp.maximum(x, 0.0) + jnp.log(1.0 + jnp.exp(-jnp.abs(x)))


def _centred_taps(xe, cw_ref, sl, tt):
    n = xe.shape[0]
    acc = cw_ref[1:2, sl] * xe[HALO:HALO + tt]
    for k in (0, 2, 3):
        shifted = pltpu.roll(xe, (n - (k - 1)) % n, axis=0)
        acc = acc + cw_ref[k:k + 1, sl] * shifted[HALO:HALO + tt]
    return acc


def _mod_kernel(c_ref, w_ref, b_ref, o_ref):
    c = c_ref[...]
    o_ref[...] = _dot3(_silu(c), w_ref[...]) + b_ref[...]


def _modulation(c8, w_mod, b_mod):
    depth, d, n = w_mod.shape
    tn = 1536
    return pl.pallas_call(
        _mod_kernel,
        grid=(depth, n // tn),
        in_specs=[pl.BlockSpec((8, d), lambda l, j: (0, 0)),
                  pl.BlockSpec((None, d, tn), lambda l, j: (l, 0, j)),
                  pl.BlockSpec((None, 1, tn), lambda l, j: (l, 0, j))],
        out_specs=pl.BlockSpec((None, 8, tn), lambda l, j: (l, 0, j)),
        out_shape=jax.ShapeDtypeStruct((depth, 8, n), F32),
        compiler_params=_cp(("parallel", "parallel")),
        name="modulation",
    )(c8, w_mod, b_mod.reshape(depth, 1, n))


S5_Q = 16
S5_PAIRS = S5_GROUPS // 2
PAIR_W = 2 * S5_Q * S5_GROUP
LB = 128
NLB = S5_WIDTH // LB


def _inproj0b_kernel(xp_ref, xc_ref, xn_ref, sc_ref, sh_ref, w_ref, cw_ref, cb_ref,
                     u_ref, xrow_ref, xcf_ref, gr_ref, us_s, *, nlt):
    i = pl.program_id(1)
    tt = TOK_TILE
    nch = tt // S5_Q
    has_prev = jnp.logical_and(i > 0, i < nlt).astype(F32)
    has_next = (i < nlt - 1).astype(F32)
    xe = jnp.concatenate([xp_ref[...], xc_ref[...], xn_ref[...]], axis=0)
    h = _norm_mod(xe, sc_ref[...], sh_ref[...]).astype(BF16)
    p = _dot(h, w_ref[...])
    u = p[HALO:HALO + tt, 0:S5_WIDTH]
    row = lax.broadcasted_iota(I32, (tt + 2 * HALO, 1), 0)
    keep = jnp.where(row < HALO, has_prev, jnp.where(row >= tt + HALO, has_next, 1.0))
    xr = p[:, S5_WIDTH:S5_WIDTH + RG_WIDTH] * keep
    xc = cb_ref[...] + _centred_taps(xr, cw_ref, slice(None), tt)
    u_ref[...] = u
    gr_ref[...] = p[HALO:HALO + tt, S5_WIDTH + RG_WIDTH:]
    xcf_ref[...] = xc
    for q in range(NLB):
        us_s[q] = u[:, LB * q:LB * (q + 1)]
    gpb = LB // S5_GROUP
    for q in range(NLB):
        steps = [us_s[q, pl.ds(s, nch, stride=S5_Q), :] for s in range(S5_Q)]
        for m in range(gpb // 2):
            halves = [jnp.concatenate([v[:, S5_GROUP * (2 * m + gl):S5_GROUP * (2 * m + gl + 1)] for v in steps],
                                      axis=1) for gl in range(2)]
            xrow_ref[(gpb // 2) * q + m] = jnp.concatenate(halves, axis=1).astype(BF16)


def _inproj0b(xall, sc1, sh1, w_in, cw8, cb, n_lat):
    b_, nt, d = xall.shape
    tt = TOK_TILE
    nch = tt // S5_Q
    nlt = n_lat // tt
    nti = nt // tt
    hb = tt // HALO
    nh = nt // HALO
    row_idx = lambda b, i: (jnp.where(i < nlt, b, b_), 0, 0)
    tok = lambda w: pl.BlockSpec((None, tt, w), lambda b, i: (b, i, 0))
    kern = functools.partial(_inproj0b_kernel, nlt=nlt)
    return pl.pallas_call(
        kern,
        grid=(b_, nti),
        in_specs=[
            pl.BlockSpec((None, HALO, d), lambda b, i: (b, jnp.maximum(i * hb - 1, 0), 0)),
            pl.BlockSpec((None, tt, d), lambda b, i: (b, i, 0)),
            pl.BlockSpec((None, HALO, d), lambda b, i: (b, jnp.minimum(i * hb + hb, nh - 1), 0)),
            pl.BlockSpec((None, 1, d), row_idx),
            pl.BlockSpec((None, 1, d), row_idx),
            pl.BlockSpec(w_in.shape, lambda b, i: (0, 0)),
            pl.BlockSpec((8, RG_WIDTH), lambda b, i: (0, 0)),
            pl.BlockSpec((1, RG_WIDTH), lambda b, i: (0, 0)),
        ],
        out_specs=[
            tok(S5_WIDTH),
            pl.BlockSpec((S5_PAIRS, nch, PAIR_W), lambda b, i: (0, i * b_ + b, 0)),
            tok(RG_WIDTH),
            tok(RG_WIDTH),
        ],
        out_shape=[
            jax.ShapeDtypeStruct((b_, nt, S5_WIDTH), F32),
            jax.ShapeDtypeStruct((S5_PAIRS, b_ * nt // S5_Q, PAIR_W), BF16),
            jax.ShapeDtypeStruct((b_, nt, RG_WIDTH), F32),
            jax.ShapeDtypeStruct((b_, nt, RG_WIDTH), F32),
        ],
        scratch_shapes=[pltpu.VMEM((NLB, tt, LB), F32)],
        compiler_params=_cp(("parallel", "parallel")),
        name="inproj0",
    )(xall, xall, xall, sc1, sh1, w_in, cw8, cb)


def _s5_local_kernel(x_ref, t_ref, wo_ref, yx_ref, fr_ref, fi_ref, br_ref, bi_ref):
    x = x_ref[...]
    yx_ref[...] = _dot(x, t_ref[...])
    sx = _dot(x, wo_ref[...])
    fr_ref[...] = sx[:, 0:LB]
    fi_ref[...] = sx[:, LB:2 * LB]
    br_ref[...] = sx[:, 2 * LB:3 * LB]
    bi_ref[...] = sx[:, 3 * LB:4 * LB]


def _s5_local(xrow, tmat, wout):
    npair, r, _ = xrow.shape
    pm = lambda w: pl.BlockSpec((None, r, w), lambda p: (p, 0, 0))
    wm = pl.BlockSpec((None, PAIR_W, PAIR_W), lambda p: (p, 0, 0))
    st = jax.ShapeDtypeStruct((npair, r, LB), F32)
    return pl.pallas_call(
        _s5_local_kernel,
        grid=(npair,),
        in_specs=[pm(PAIR_W), wm, wm],
        out_specs=[pm(PAIR_W), pm(LB), pm(LB), pm(LB), pm(LB)],
        out_shape=[jax.ShapeDtypeStruct((npair, r, PAIR_W), F32), st, st, st, st],
        compiler_params=_cp(("parallel",)),
        name="s5_local",
    )(xrow, tmat, wout)


def _s5_state_out_kernel(hr_ref, hi_ref, gr_ref, gi_ref, win_ref, yx_ref, o_ref):
    hg = jnp.concatenate([hr_ref[...], hi_ref[...], gr_ref[...], gi_ref[...]], axis=1).astype(BF16)
    o_ref[...] = yx_ref[...] + _dot(hg, win_ref[...])


def _s5_state_out(hr, hi, gr, gi, win, yx):
    npair, r, _ = yx.shape
    pm = lambda w: pl.BlockSpec((None, r, w), lambda p: (p, 0, 0))
    return pl.pallas_call(
        _s5_state_out_kernel,
        grid=(npair,),
        in_specs=[pm(LB), pm(LB), pm(LB), pm(LB), pl.BlockSpec((None, PAIR_W, PAIR_W), lambda p: (p, 0, 0)),
                  pm(PAIR_W)],
        out_specs=pm(PAIR_W),
        out_shape=jax.ShapeDtypeStruct((npair, r, PAIR_W), F32),
        input_output_aliases={5: 0},
        compiler_params=_cp(("parallel",)),
        name="s5_state_out",
    )(hr, hi, gr, gi, win, yx)


def _scan0b_kernel(fr_ref, fi_ref, br_ref, bi_ref, xcf_ref, xcb_ref, aq_ref, wa_ref, wx_ref, ba_ref, bx_ref,
                   sp_ref, hr_ref, hi_ref, gr_ref, gi_ref, hgf_ref, hgb_ref, st_s, ga_s, gb_s, hg_s):
    j = pl.program_id(0)
    nb, tt, _ = xcf_ref.shape
    nch = tt // S5_Q

    @pl.when(j == 0)
    def _():
        st_s[...] = jnp.zeros_like(st_s)
        hg_s[...] = jnp.zeros_like(hg_s)

    for d, (sr_ref, si_ref, or_ref, oi_ref) in enumerate(((fr_ref, fi_ref, hr_ref, hi_ref),
                                                          (br_ref, bi_ref, gr_ref, gi_ref))):
        for k in range(nch):
            c = k if d == 0 else nch - 1 - k
            rows = pl.ds(c, nb, stride=nch)
            for p in range(S5_PAIRS):
                hr = st_s[d, 0, p]
                hi = st_s[d, 1, p]
                or_ref[p, rows, :] = hr
                oi_ref[p, rows, :] = hi
                ar = aq_ref[d, 0, p]
                ai = aq_ref[d, 1, p]
                st_s[d, 0, p] = ar * hr - ai * hi + sr_ref[p, rows, :]
                st_s[d, 1, p] = ar * hi + ai * hr + si_ref[p, rows, :]

    sub = jnp.bitwise_and(lax.broadcasted_iota(I32, (tt, RG_WIDTH), 0), 7)
    ntl = tt // 8
    for d, x_ref in enumerate((xcf_ref, xcb_ref)):
        for b in range(nb):
            x = x_ref[b]
            xb = x.astype(BF16)
            log_a = (-RG_C) * jax.nn.sigmoid(_dot(xb, wa_ref[d]) + ba_ref[d]) * sp_ref[d]
            a = jnp.exp(log_a)
            bv = jnp.sqrt(1.0 - a * a) * jax.nn.sigmoid(_dot(xb, wx_ref[d]) + bx_ref[d]) * x
            for s in (1, 2, 4):
                if d == 0:
                    ok = sub >= s
                    shift = s
                else:
                    ok = sub < 8 - s
                    shift = tt - s
                a_prev = jnp.where(ok, pltpu.roll(a, shift, axis=0), 1.0)
                b_prev = jnp.where(ok, pltpu.roll(bv, shift, axis=0), 0.0)
                bv = bv + a * b_prev
                a = a * a_prev
            ga_s[d, b] = a
            gb_s[d, b] = bv

    def tile_step(k, carry):
        out = []
        for d in range(2):
            r0 = pl.multiple_of((k if d == 0 else ntl - 1 - k) * 8, 8)
            for b in range(nb):
                h = gb_s[d, b, pl.ds(r0, 8), :] + ga_s[d, b, pl.ds(r0, 8), :] * carry[d * nb + b]
                gb_s[d, b, pl.ds(r0, 8), :] = h
                last = h[7:8, :] if d == 0 else h[0:1, :]
                out.append(jnp.broadcast_to(last, (8, RG_WIDTH)))
        return tuple(out)

    init = tuple(hg_s[d, b] for d in range(2) for b in range(nb))
    fin = lax.fori_loop(0, ntl, tile_step, init)
    for d in range(2):
        for b in range(nb):
            hg_s[d, b] = fin[d * nb + b]
    for d, o_ref in enumerate((hgf_ref, hgb_ref)):
        for b in range(nb):
            o_ref[b] = gb_s[d, b]


def _scan0b(sfr, sfi, sbr, sbi, xc, aq, wa, wx, ba, bx, sp, n_lat):
    b_, nt, _ = xc.shape
    tt = TOK_TILE
    nch = tt // S5_Q
    ntile = nt // tt
    nlt = n_lat // tt
    nct = ntile - nlt
    tf = lambda j: jnp.where(j < nct, nlt + j, j - nct)
    tb = lambda j: ntile - 1 - j
    rblk = b_ * nch
    sf = pl.BlockSpec((S5_PAIRS, rblk, LB), lambda j: (0, tf(j), 0))
    sb = pl.BlockSpec((S5_PAIRS, rblk, LB), lambda j: (0, tb(j), 0))
    xf = pl.BlockSpec((b_, tt, RG_WIDTH), lambda j: (0, tf(j), 0))
    xb = pl.BlockSpec((b_, tt, RG_WIDTH), lambda j: (0, tb(j), 0))
    full = lambda a: pl.BlockSpec(a.shape, lambda j: (0,) * a.ndim)
    st = jax.ShapeDtypeStruct(sfr.shape, F32)
    hg = jax.ShapeDtypeStruct(xc.shape, F32)
    return pl.pallas_call(
        _scan0b_kernel,
        grid=(ntile,),
        in_specs=[sf, sf, sb, sb, xf, xb, full(aq), full(wa), full(wx), full(ba), full(bx), full(sp)],
        out_specs=[sf, sf, sb, sb, xf, xb],
        out_shape=[st, st, st, st, hg, hg],
        scratch_shapes=[
            pltpu.VMEM((2, 2, S5_PAIRS, b_, LB), F32),
            pltpu.VMEM((2, b_, tt, RG_WIDTH), F32),
            pltpu.VMEM((2, b_, tt, RG_WIDTH), F32),
            pltpu.VMEM((2, b_, 8, RG_WIDTH), F32),
        ],
        compiler_params=_cp(("arbitrary",)),
        name="scan0",
    )(sfr, sfi, sbr, sbi, xc, xc, aq, wa, wx, ba, bx, sp)


def _merge0b_kernel(yx_ref, hf_ref, hb_ref, u_ref, gr_ref, x_ref, g1_ref, d_ref, wglu_ref, wout_ref, o_ref, ys_s):
    nch = yx_ref.shape[1]
    gpb = LB // S5_GROUP
    rows = [yx_ref[p] for p in range(S5_PAIRS)]
    for t in range(S5_Q):
        for q in range(NLB):
            pieces = []
            for m in range(gpb // 2):
                r = rows[(gpb // 2) * q + m]
                for gl in range(2):
                    o = gl * S5_Q * S5_GROUP + t * S5_GROUP
                    pieces.append(r[:, o:o + S5_GROUP])
            ys_s[q, pl.ds(t, nch, stride=S5_Q), :] = jnp.concatenate(pieces, axis=1)
    y = jnp.concatenate([ys_s[q] for q in range(NLB)], axis=1) + d_ref[...] * u_ref[...]
    y = _gelu(y)
    y = y * jax.nn.sigmoid(_dot(y.astype(BF16), wglu_ref[...]))
    g = (hf_ref[...] + hb_ref[...]) * _gelu(gr_ref[...])
    m = _dot(jnp.concatenate([y, g], axis=1).astype(BF16), wout_ref[...])
    o_ref[...] = x_ref[...] + g1_ref[...] * m


def _merge0b(yx, hgf, hgb, u, gr, xall, g1, s5d, wglu, wout, n_lat):
    b_, nt, d = xall.shape
    tt = TOK_TILE
    nch = tt // S5_Q
    nlt = n_lat // tt
    row_idx = lambda b, i: (jnp.where(i < nlt, b, b_), 0, 0)
    tok = lambda w: pl.BlockSpec((None, tt, w), lambda b, i: (b, i, 0))
    const = lambda b, i: (0, 0)
    return pl.pallas_call(
        _merge0b_kernel,
        grid=(b_, nt // tt),
        in_specs=[
            pl.BlockSpec((S5_PAIRS, nch, PAIR_W), lambda b, i: (0, i * b_ + b, 0)),
            tok(RG_WIDTH), tok(RG_WIDTH), tok(S5_WIDTH), tok(RG_WIDTH), tok(d),
            pl.BlockSpec((None, 1, d), row_idx),
            pl.BlockSpec((1, S5_WIDTH), const),
            pl.BlockSpec(wglu.shape, const),
            pl.BlockSpec(wout.shape, const),
        ],
        out_specs=tok(d),
        out_shape=jax.ShapeDtypeStruct((b_, nt, d), F32),
        scratch_shapes=[pltpu.VMEM((NLB, tt, LB), F32)],
        compiler_params=_cp(("parallel", "parallel")),
        name="merge0",
    )(yx, hgf, hgb, u, gr, xall, g1, s5d, wglu, wout)


def _s5_chunk_params(lam_re, lam_im, log_dt, b_re, b_im, c_re, c_im, nb):
    hp = lax.Precision.HIGHEST
    qn = S5_Q
    lam_re = lam_re.astype(F32)
    lam_im = lam_im.astype(F32)
    dt = jnp.exp(log_dt.astype(F32))[..., None]
    n = jnp.arange(qn + 1, dtype=F32)[:, None, None, None]
    mag = jnp.exp(n * (lam_re * dt)[None])
    ang = n * (lam_im * dt)[None]
    zr = mag * jnp.cos(ang)
    zi = mag * jnp.sin(ang)
    ar, ai = zr[1], zi[1]
    den = lam_re * lam_re + lam_im * lam_im
    fr = ((ar - 1.0) * lam_re + ai * lam_im) / den
    fi = (ai * lam_re - (ar - 1.0) * lam_im) / den
    bbr = fr[..., None] * b_re - fi[..., None] * b_im
    bbi = fr[..., None] * b_im + fi[..., None] * b_re
    cr = c_re.astype(F32)
    ci = c_im.astype(F32)
    wr = zr[:qn, ..., None] * bbr[None] - zi[:qn, ..., None] * bbi[None]
    wi = zr[:qn, ..., None] * bbi[None] + zi[:qn, ..., None] * bbr[None]
    kern = (jnp.einsum('dgkp,ndgpj->ndgkj', cr, wr, precision=hp)
            - jnp.einsum('dgkp,ndgpj->ndgkj', ci, wi, precision=hp))
    steps = jnp.arange(qn)
    lag = steps[None, :] - steps[:, None]
    lags = jnp.arange(qn)[:, None, None]
    sel_f = (lag[None] == lags).astype(F32)
    sel_b = (-lag[None] == lags).astype(F32)
    eye2 = jnp.eye(2, dtype=F32)

    def pair_diag(m):
        g, r, c = m.shape
        z = m.reshape(g // 2, 2, r, 1, c) * eye2[None, :, None, :, None]
        return z.reshape(g // 2, 2 * r, 2 * c)

    tg = (jnp.einsum('lst,lgkj->gsjtk', sel_f, kern[:, 0], precision=hp)
          + jnp.einsum('lst,lgkj->gsjtk', sel_b, kern[:, 1], precision=hp))
    tmat = pair_diag(tg.reshape(S5_GROUPS, qn * S5_GROUP, qn * S5_GROUP)).astype(BF16)
    def out_part(w, d, flip):
        m = w[::-1, d] if flip else w[:, d]
        return m.transpose(1, 0, 3, 2).reshape(S5_GROUPS, qn * S5_GROUP, S5_STATE)
    parts = [out_part(wr, 0, True), out_part(wi, 0, True), out_part(wr, 1, False), out_part(wi, 1, False)]
    wout = jnp.concatenate([pair_diag(pp) for pp in parts], axis=2).astype(BF16)
    def in_parts(d, powers):
        zr_p = zr[powers, d]
        zi_p = zi[powers, d]
        m_re = (cr[d][None] * zr_p[:, :, None, :] - ci[d][None] * zi_p[:, :, None, :])
        m_im = -(cr[d][None] * zi_p[:, :, None, :] + ci[d][None] * zr_p[:, :, None, :])
        f = lambda m: m.transpose(1, 3, 0, 2).reshape(S5_GROUPS, S5_STATE, qn * S5_GROUP)
        return f(m_re), f(m_im)
    fre, fim = in_parts(0, jnp.arange(1, qn + 1))
    bre, bim = in_parts(1, qn - jnp.arange(qn))
    win = jnp.concatenate([pair_diag(pp) for pp in (fre, fim, bre, bim)], axis=1).astype(BF16)
    aq = jnp.stack([zr[qn], zi[qn]], axis=1).reshape(2, 2, S5_PAIRS, 1, LB)
    return tmat, wout, win, jnp.broadcast_to(aq, (2, 2, S5_PAIRS, nb, LB))


def _rg_params_b(wa, ba, wx, bx, lam):
    eye = jnp.eye(RG_HEADS, dtype=F32)
    dense = lambda w: jnp.einsum('dhij,hk->dhikj', w.astype(F32), eye).reshape(2, RG_WIDTH, RG_WIDTH).astype(BF16)
    row = lambda v: v.astype(F32).reshape(2, 1, RG_WIDTH)
    return dense(wa), dense(wx), row(ba), row(bx), row(jax.nn.softplus(-lam.astype(F32)))


def _route_kernel(x_ref, sc_ref, sh_ref, rwh_ref, rwl_ref, rb_ref, hb_ref, meta_ref, cnt_ref, carry_s):
    first = jnp.logical_and(pl.program_id(0) == 0, pl.program_id(1) == 0)

    @pl.when(first)
    def _():
        carry_s[...] = jnp.zeros_like(carry_s)

    tt = TOK_TILE
    h = _norm_mod(x_ref[...], sc_ref[...], sh_ref[...])
    hh, hl = _split2(h)
    logits = _dot(hh, rwh_ref[...]) + (_dot(hl, rwh_ref[...]) + _dot(hh, rwl_ref[...]))
    lane = lax.broadcasted_iota(I32, (tt, 128), 1)
    valid = lane < N_EXPERTS
    neg = -1e30
    lg = jnp.where(valid, logits, neg)
    mx = jnp.max(lg, axis=-1, keepdims=True)
    ex = jnp.where(valid, jnp.exp(lg - mx), 0.0)
    probs = ex / jnp.sum(ex, axis=-1, keepdims=True)
    sel = probs + rb_ref[...]
    grp = jnp.right_shift(lane, 2)
    best = jnp.zeros((tt, 1), I32)
    bestv = jnp.max(jnp.where(jnp.logical_and(valid, grp == 0), sel, neg), axis=-1, keepdims=True)
    for k in range(1, N_EXPERT_GROUPS):
        gk = jnp.max(jnp.where(jnp.logical_and(valid, grp == k), sel, neg), axis=-1, keepdims=True)
        upd = gk > bestv
        best = jnp.where(upd, k, best)
        bestv = jnp.where(upd, gk, bestv)
    msel = jnp.where(jnp.logical_and(valid, grp == best), sel, neg)
    v1 = jnp.max(msel, axis=-1, keepdims=True)
    i1 = jnp.min(jnp.where(msel == v1, lane, 128), axis=-1, keepdims=True)
    msel2 = jnp.where(lane == i1, neg, msel)
    v2 = jnp.max(msel2, axis=-1, keepdims=True)
    i2 = jnp.min(jnp.where(msel2 == v2, lane, 128), axis=-1, keepdims=True)
    p1 = jnp.sum(jnp.where(lane == i1, probs, 0.0), axis=-1, keepdims=True)
    p2 = jnp.sum(jnp.where(lane == i2, probs, 0.0), axis=-1, keepdims=True)
    den = p1 + p2
    g1 = p1 / den
    g2 = p2 / den
    lo = jnp.minimum(i1, i2) - EXPERTS_PER_GROUP * best
    hi = jnp.maximum(i1, i2) - EXPERTS_PER_GROUP * best
    g_lo = jnp.where(i1 < i2, g1, g2)
    g_hi = jnp.where(i1 < i2, g2, g1)
    pair = jnp.right_shift(lo * (7 - lo), 1) + (hi - lo - 1)
    cls = best * 6 + pair

    onehot = jnp.where(lane == cls, 1.0, 0.0)
    r_i = lax.broadcasted_iota(I32, (tt, tt), 0)
    c_i = lax.broadcasted_iota(I32, (tt, tt), 1)
    before = jnp.where(c_i < r_i, 1.0, 0.0).astype(BF16)
    cum = _dot(before, onehot.astype(BF16))
    rank = jnp.sum(onehot * (cum + carry_s[...]), axis=-1, keepdims=True)
    carry_s[...] = carry_s[...] + jnp.sum(onehot, axis=0, keepdims=True)

    hb_ref[:, 0:D_MODEL] = h
    hb_ref[:, D_MODEL:] = jnp.where(lane == 0, g_lo, jnp.where(lane == 1, g_hi, 0.0))
    meta = jnp.where(lane == 0, cls.astype(F32), jnp.where(lane == 1, rank, 0.0))
    meta_ref[...] = meta.T[0:8, :].astype(I32)
    cnt_ref[...] = jnp.broadcast_to(carry_s[...], (8, 128))


def _route(x, sc2, sh2, rwh, rwl, rb, n_lat):
    b_, nt, d = x.shape
    tt = TOK_TILE
    nlt = n_lat // tt
    nti = nt // tt
    row_idx = lambda b, i: (jnp.where(i < nlt, b, b_), 0, 0)
    flat = lambda b, i: (b * nti + i, 0)
    return pl.pallas_call(
        _route_kernel,
        grid=(b_, nti),
        in_specs=[
            pl.BlockSpec((None, tt, d), lambda b, i: (b, i, 0)),
            pl.BlockSpec((None, 1, d), row_idx),
            pl.BlockSpec((None, 1, d), row_idx),
            pl.BlockSpec((d, 128), lambda b, i: (0, 0)),
            pl.BlockSpec((d, 128), lambda b, i: (0, 0)),
            pl.BlockSpec((1, 128), lambda b, i: (0, 0)),
        ],
        out_specs=[
            pl.BlockSpec((tt, ROW_W), flat),
            pl.BlockSpec((None, 8, tt), lambda b, i: (b * nti + i, 0, 0)),
            pl.BlockSpec((8, 128), lambda b, i: (0, 0)),
        ],
        out_shape=[
            jax.ShapeDtypeStruct((b_ * nt, ROW_W), F32),
            jax.ShapeDtypeStruct((b_ * nti, 8, tt), I32),
            jax.ShapeDtypeStruct((8, 128), F32),
        ],
        scratch_shapes=[pltpu.VMEM((1, 128), F32)],
        compiler_params=_cp(("arbitrary", "arbitrary")),
        name="route",
    )(x, sc2, sh2, rwh, rwl, rb)


def _row_copy(src, dst, src_row, dst_row, sem):
    return pltpu.make_async_copy(src.at[pl.ds(src_row, 1)], dst.at[pl.ds(dst_row, 1)], sem)


def _dest_row(ps_ref, meta_ref, r):
    return ps_ref[meta_ref[0, r]] + meta_ref[1, r]


ISSUE_UNROLL = 8


def _issue_rows(n, start_row):
    def body(j, carry):
        for k in range(ISSUE_UNROLL):
            start_row(j * ISSUE_UNROLL + k, k % 2)
        return carry

    lax.fori_loop(0, n // ISSUE_UNROLL, body, 0)


def _dispatch_kernel(ps_ref, meta_ref, hb_ref, xb_in, xb_out, ring, fsems, sems):
    del xb_in
    tt = TOK_TILE
    i = pl.program_id(0)
    n = pl.num_programs(0)
    slot = lax.rem(i, 2)
    rslot = lax.rem(i, 3)

    def fetch(t, s):
        return pltpu.make_async_copy(hb_ref.at[pl.ds(t * tt, tt)], ring.at[s], fsems.at[s])

    @pl.when(i == 0)
    def _():
        fetch(0, 0).start()

    @pl.when(i + 1 < n)
    def _():
        fetch(i + 1, lax.rem(i + 1, 3)).start()

    fetch(i, rslot).wait()
    src = ring.at[rslot]

    def start_row(r, prio):
        _row_copy(src, xb_out, r, _dest_row(ps_ref, meta_ref, r), sems.at[slot]).start(priority=prio)

    _issue_rows(tt, start_row)

    def drain(s):
        pltpu.make_async_copy(src, xb_out.at[pl.ds(0, tt)], sems.at[s]).wait()

    @pl.when(i > 0)
    def _():
        drain(1 - slot)

    @pl.when(i == n - 1)
    def _():
        drain(slot)


def _dispatch(pstart, meta, hb, n_rows):
    t = hb.shape[0]
    tt = TOK_TILE
    zeros = jnp.zeros((n_rows, ROW_W), F32)
    gs = pltpu.PrefetchScalarGridSpec(
        num_scalar_prefetch=1,
        grid=(t // tt,),
        in_specs=[
            pl.BlockSpec((None, 8, tt), lambda i, ps: (i, 0, 0), memory_space=pltpu.SMEM),
            pl.BlockSpec(memory_space=pl.ANY),
            pl.BlockSpec(memory_space=pl.ANY),
        ],
        out_specs=pl.BlockSpec(memory_space=pl.ANY),
        scratch_shapes=[pltpu.VMEM((3, tt, ROW_W), F32), pltpu.SemaphoreType.DMA((3,)),
                        pltpu.SemaphoreType.DMA((2,))],
    )
    return pl.pallas_call(
        _dispatch_kernel,
        grid_spec=gs,
        out_shape=jax.ShapeDtypeStruct((n_rows, ROW_W), F32),
        input_output_aliases={3: 0},
        compiler_params=_cp(("arbitrary",)),
        name="dispatch",
    )(pstart, meta, hb, zeros)


def _expert_kernel(ea_ref, eb_ref, nu_ref, x_ref, w1a, w1b, w3a, w3b, w2a, w2b, o_ref):
    del ea_ref, eb_ref
    j = pl.program_id(0)

    @pl.when(j < nu_ref[0])
    def _():
        x = x_ref[:, 0:D_MODEL].astype(BF16)
        g_lo = x_ref[:, D_MODEL:D_MODEL + 1]
        g_hi = x_ref[:, D_MODEL + 1:D_MODEL + 2]

        def ffn(w1, w3, w2):
            a = _dot(x, w1[...])
            b = _dot(x, w3[...])
            return _dot((_silu(a) * b).astype(BF16), w2[...])

        o_ref[...] = g_lo * ffn(w1a, w3a, w2a) + g_hi * ffn(w1b, w3b, w2b)

    @pl.when(j >= nu_ref[0])
    def _():
        o_ref[...] = jnp.zeros_like(o_ref)


def _experts(blk_ea, blk_eb, nused, xb, w1, w3, w2, layer):
    n_rows = xb.shape[0]
    nb = n_rows // MOE_BLOCK
    d = D_MODEL
    wa = lambda j, ea, eb, nu: (layer, ea[j], 0, 0)
    wb = lambda j, ea, eb, nu: (layer, eb[j], 0, 0)
    wspec = lambda f: pl.BlockSpec((None, None, d, d), f)
    gs = pltpu.PrefetchScalarGridSpec(
        num_scalar_prefetch=3,
        grid=(nb,),
        in_specs=[pl.BlockSpec((MOE_BLOCK, ROW_W), lambda j, ea, eb, nu: (j, 0)),
                  wspec(wa), wspec(wb), wspec(wa), wspec(wb), wspec(wa), wspec(wb)],
        out_specs=pl.BlockSpec((MOE_BLOCK, d), lambda j, ea, eb, nu: (j, 0)),
    )
    return pl.pallas_call(
        _expert_kernel,
        grid_spec=gs,
        out_shape=jax.ShapeDtypeStruct((n_rows, d), F32),
        compiler_params=_cp(("arbitrary",)),
        name="experts",
    )(blk_ea, blk_eb, nused, xb, w1, w1, w3, w3, w2, w2)


def _combine_kernel(ps_ref, meta_ref, metan_ref, yb_ref, x_ref, g2_ref, nw_ref, o_ref, buf, sems, *, final):
    tt = TOK_TILE
    i = pl.program_id(0)
    n = pl.num_programs(0)
    slot = lax.rem(i, 2)

    def gather(mref, s):
        def start_row(r, prio):
            _row_copy(yb_ref, buf.at[s], _dest_row(ps_ref, mref, r), r, sems.at[s]).start(priority=prio)

        _issue_rows(tt, start_row)

    @pl.when(i == 0)
    def _():
        gather(meta_ref, 0)

    @pl.when(i + 1 < n)
    def _():
        gather(metan_ref, 1 - slot)

    pltpu.make_async_copy(yb_ref.at[pl.ds(0, tt)], buf.at[slot], sems.at[slot]).wait()
    x2 = x_ref[...] + g2_ref[...] * buf[slot]
    if final:
        ms = jnp.mean(x2 * x2, axis=-1, keepdims=True)
        x2 = x2 * lax.rsqrt(ms + EPS) * nw_ref[...]
    o_ref[...] = x2


def _combine(pstart, meta, yb, x, g2, nw, n_lat, final):
    b_, nt, d = x.shape
    tt = TOK_TILE
    nlt = n_lat // tt
    nti = nt // tt
    ntile = b_ * nti
    row_idx = lambda i, ps: (jnp.where(lax.rem(i, nti) < nlt, i // nti, b_), 0, 0)
    kern = functools.partial(_combine_kernel, final=final)
    gs = pltpu.PrefetchScalarGridSpec(
        num_scalar_prefetch=1,
        grid=(ntile,),
        in_specs=[
            pl.BlockSpec((None, 8, tt), lambda i, ps: (i, 0, 0), memory_space=pltpu.SMEM),
            pl.BlockSpec((None, 8, tt), lambda i, ps: (jnp.minimum(i + 1, ntile - 1), 0, 0),
                         memory_space=pltpu.SMEM),
            pl.BlockSpec(memory_space=pl.ANY),
            pl.BlockSpec((tt, d), lambda i, ps: (i, 0)),
            pl.BlockSpec((None, 1, d), row_idx),
            pl.BlockSpec((1, d), lambda i, ps: (0, 0)),
        ],
        out_specs=pl.BlockSpec((tt, d), lambda i, ps: (i, 0)),
        scratch_shapes=[pltpu.VMEM((2, tt, d), F32), pltpu.SemaphoreType.DMA((2,))],
    )
    out = pl.pallas_call(
        kern,
        grid_spec=gs,
        out_shape=jax.ShapeDtypeStruct((b_ * nt, d), F32),
        compiler_params=_cp(("arbitrary",)),
        name="combine",
    )(pstart, meta, meta, yb, x.reshape(b_ * nt, d), g2, nw)
    return out.reshape(b_, nt, d)


def _class_tables():
    ea, eb = [], []
    for g in range(N_EXPERT_GROUPS):
        for lo in range(EXPERTS_PER_GROUP):
            for hi in range(lo + 1, EXPERTS_PER_GROUP):
                ea.append(g * EXPERTS_PER_GROUP + lo)
                eb.append(g * EXPERTS_PER_GROUP + hi)
    return jnp.array(ea, I32), jnp.array(eb, I32)


def _moe(x, sc2, sh2, g2, rwh, rwl, rb, w1, w3, w2, layer, nw, n_lat, final):
    b_, nt, d = x.shape
    t = b_ * nt
    hb, meta, cnt = _route(x, sc2, sh2, rwh, rwl, rb, n_lat)
    counts = cnt[0, :N_CLASSES].astype(I32)
    padded = (counts + MOE_BLOCK - 1) // MOE_BLOCK * MOE_BLOCK
    pend = jnp.cumsum(padded)
    pstart = pend - padded
    nb = t // MOE_BLOCK + N_CLASSES
    blk_start = jnp.arange(nb, dtype=I32)[:, None] * MOE_BLOCK
    blk_cls = jnp.minimum(jnp.sum((pend[None, :] <= blk_start).astype(I32), axis=1), N_CLASSES - 1)
    ea, eb = _class_tables()
    nused = (pend[-1] // MOE_BLOCK).reshape(1).astype(I32)
    xb = _dispatch(pstart, meta, hb, nb * MOE_BLOCK)
    onehot = (blk_cls[:, None] == jnp.arange(N_CLASSES, dtype=I32)[None, :]).astype(I32)
    yb = _experts(onehot @ ea, onehot @ eb, nused, xb, w1, w3, w2, layer)
    return _combine(pstart, meta, yb, x, g2, nw, n_lat, final)


def _inproj1_kernel(*refs, latent, nq):
    if latent:
        (xp_ref, xm_ref, xn_ref, sc_ref, sh_ref, wz_ref, wx_ref, wd_ref, cw_ref, cb_ref, db_ref,
         z_ref, xbc_ref, dt_ref, scr) = refs
        q = pl.program_id(1)
        has_prev = (q > 0).astype(F32)
        has_next = (q < nq - 1).astype(F32)
        cols = [xm_ref[k] for k in range(xm_ref.shape[0])]
        xe = jnp.concatenate([xp_ref[...]] + cols + [xn_ref[...]], axis=0)
    else:
        (xm_ref, sc_ref, sh_ref, wz_ref, wx_ref, wd_ref, cw_ref, cb_ref, db_ref,
         z_ref, xbc_ref, dt_ref, scr) = refs
        has_prev = 0.0
        has_next = 0.0
        pad = jnp.zeros((HALO, D_MODEL), F32)
        xe = jnp.concatenate([pad, xm_ref[...], pad], axis=0)
    tt = z_ref.shape[0]
    h = _norm_mod(xe, sc_ref[...], sh_ref[...]).astype(BF16)
    hc = h[HALO:HALO + tt]
    z_ref[...] = _dot(hc, wz_ref[...]).astype(BF16)
    row = lax.broadcasted_iota(I32, (tt + 2 * HALO, 1), 0)
    keep = jnp.where(row < HALO, has_prev, jnp.where(row >= tt + HALO, has_next, 1.0))
    pw = 1024
    for pc in range(M2_CONV_DIM // pw):
        sl = slice(pc * pw, (pc + 1) * pw)
        scr[...] = _dot(h, wx_ref[:, sl]) * keep
        acc = cb_ref[:, sl] + cw_ref[0:1, sl] * scr[pl.ds(HALO - 1, tt), :]
        for k in range(1, CONV_K):
            acc = acc + cw_ref[k:k + 1, sl] * scr[pl.ds(HALO - 1 + k, tt), :]
        xbc_ref[:, sl] = _silu(acc).astype(BF16)
    lane = lax.broadcasted_iota(I32, (tt, 128), 1)
    dt = _softplus(_dot(hc, wd_ref[...]) + db_ref[...])
    dt_ref[...] = jnp.where(lane < 2 * M2_HEADS, dt, 0.0)


def _inproj1(xall, xcol, sc1, sh1, wz, wx, wd, cw8, cb, db, n_lat, latent):
    b_, nt, d = xall.shape
    if latent:
        n = n_lat
        rows = n_lat // GRID_W
        nc = INPROJ1_COLS
        tt = nc * rows
        nq = GRID_W // nc
        rb = rows // HALO
        x_specs = [
            pl.BlockSpec((None, None, HALO, d), lambda b, q: (b, jnp.maximum(nc * q - 1, 0), rb - 1, 0)),
            pl.BlockSpec((None, nc, rows, d), lambda b, q: (b, q, 0, 0)),
            pl.BlockSpec((None, None, HALO, d), lambda b, q: (b, jnp.minimum(nc * q + nc, GRID_W - 1), 0, 0)),
        ]
        x_args = [xcol, xcol, xcol]
        row_idx = lambda b, q: (b, 0, 0)
    else:
        tt = TOK_TILE
        n = nt - n_lat
        nq = n // tt
        off = n_lat // tt
        x_specs = [pl.BlockSpec((None, tt, d), lambda b, q: (b, off + q, 0))]
        x_args = [xall]
        row_idx = lambda b, q: (b_, 0, 0)
    const = lambda b, q: (0, 0)
    kern = functools.partial(_inproj1_kernel, latent=latent, nq=nq)
    return pl.pallas_call(
        kern,
        grid=(b_, nq),
        in_specs=x_specs + [
            pl.BlockSpec((None, 1, d), row_idx),
            pl.BlockSpec((None, 1, d), row_idx),
            pl.BlockSpec(wz.shape, const, pipeline_mode=pl.Buffered(1)),
            pl.BlockSpec(wx.shape, const, pipeline_mode=pl.Buffered(1)),
            pl.BlockSpec(wd.shape, const, pipeline_mode=pl.Buffered(1)),
            pl.BlockSpec(cw8.shape, const),
            pl.BlockSpec(cb.shape, const),
            pl.BlockSpec(db.shape, const),
        ],
        out_specs=[
            pl.BlockSpec((None, tt, M2_INNER), lambda b, q: (b, q, 0)),
            pl.BlockSpec((None, tt, M2_CONV_DIM), lambda b, q: (b, q, 0)),
            pl.BlockSpec((None, tt, 128), lambda b, q: (b, q, 0)),
        ],
        out_shape=[
            jax.ShapeDtypeStruct((b_, n, M2_INNER), BF16),
            jax.ShapeDtypeStruct((b_, n, M2_CONV_DIM), BF16),
            jax.ShapeDtypeStruct((b_, n, 128), F32),
        ],
        scratch_shapes=[pltpu.VMEM((tt + 2 * HALO, 1024), F32)],
        compiler_params=_cp(("parallel", "parallel")),
        name="inproj1_lat" if latent else "inproj1_ctx",
    )(*x_args, sc1, sh1, wz, wx, wd, cw8, cb, db)


def _ssd_kernel(*refs, reverse, need_y, lane0):
    if need_y:
        xbc_ref, dt_ref, arow_ref, e_ref, s0_ref, y_ref, sf_ref, s_s = refs
    else:
        xbc_ref, dt_ref, arow_ref, e_ref, s0_ref, sf_ref, s_s = refs
    c = pl.program_id(1)
    nc = pl.num_programs(1)
    q = M2_CHUNK

    @pl.when(c == 0)
    def _():
        s_s[...] = s0_ref[...]

    dt = dt_ref[...]
    a = dt * arow_ref[...]
    row = lax.broadcasted_iota(I32, (q, q), 0)
    col = lax.broadcasted_iota(I32, (q, q), 1)
    incl = (col >= row) if reverse else (col <= row)
    incl_t = (row >= col) if reverse else (row <= col)
    lt = jnp.where(incl, 1.0, 0.0).astype(BF16)
    lt_t = jnp.where(incl_t, 1.0, 0.0).astype(BF16)
    a1, a2, a3 = _split3(a)
    cum = _dot(lt, a1) + (_dot(lt, a2) + _dot(lt, a3))
    b1, b2, b3 = _split3(a.T)
    cum_t = _dot(b1, lt_t) + (_dot(b2, lt_t) + _dot(b3, lt_t))
    atot = cum[0:1, :] if reverse else cum[q - 1:q, :]
    wexp = _dot((dt * jnp.exp(atot - cum)).astype(BF16), e_ref[...])
    dt_t = dt.T
    eh, el = _split2(jnp.broadcast_to(jnp.exp(atot), (8, 128)))
    atx = _dot(eh, e_ref[...]) + _dot(el, e_ref[...])

    for g in range(M2_GROUPS):
        bg = xbc_ref[:, M2_INNER + M2_STATE * g:M2_INNER + M2_STATE * (g + 1)]
        gw = M2_HPG * M2_HEAD_DIM
        xg = xbc_ref[:, gw * g:gw * (g + 1)]
        sg = s_s[g]
        if need_y:
            cg = xbc_ref[:, M2_INNER + M2_GROUPS * M2_STATE + M2_STATE * g:
                         M2_INNER + M2_GROUPS * M2_STATE + M2_STATE * (g + 1)]
            cb = lax.dot_general(cg, bg, (((1,), (1,)), ((), ())), preferred_element_type=F32)
            cg32 = cg.astype(F32)
            sgb = sg.astype(BF16)
            y_parts = []
            for r in range(M2_HPG):
                hl = lane0 + M2_HPG * g + r
                ps = slice(M2_HEAD_DIM * r, M2_HEAD_DIM * (r + 1))
                colb = jnp.broadcast_to(cum[:, hl:hl + 1], (q, q))
                rowb = jnp.broadcast_to(cum_t[hl:hl + 1, :], (q, q))
                lmat = jnp.exp(jnp.where(incl, colb - rowb, -1e30))
                gm = (cb * lmat * dt_t[hl:hl + 1, :]).astype(BF16)
                cd = (jnp.exp(colb) * cg32).astype(BF16)
                lhs = jnp.concatenate([gm, cd], axis=1)
                rhs = jnp.concatenate([xg[:, ps], sgb[:, ps]], axis=0)
                y_parts.append(_dot(lhs, rhs))
        xw = (xg.astype(F32) * wexp[:, gw * g:gw * (g + 1)]).astype(BF16)
        snew = lax.dot_general(bg, xw, (((0,), (0,)), ((), ())), preferred_element_type=F32)
        s_s[g] = atx[0:1, gw * g:gw * (g + 1)] * sg + snew
        if need_y:
            y_ref[:, gw * g:gw * (g + 1)] = jnp.concatenate(y_parts, axis=1).astype(BF16)

    @pl.when(c == nc - 1)
    def _():
        sf_ref[...] = s_s[...]


def _ssd(xbc, dt, arow, emat, s0, reverse, need_y):
    b_, n, _ = xbc.shape
    q = M2_CHUNK
    nc = n // q
    cidx = (lambda c: nc - 1 - c) if reverse else (lambda c: c)
    sshape = (M2_GROUPS, M2_STATE, M2_HPG * M2_HEAD_DIM)
    kern = functools.partial(_ssd_kernel, reverse=reverse, need_y=need_y, lane0=M2_HEADS if reverse else 0)
    out_specs = [pl.BlockSpec((None,) + sshape, lambda b, c: (b, 0, 0, 0))]
    out_shape = [jax.ShapeDtypeStruct((b_,) + sshape, F32)]
    if need_y:
        out_specs = [pl.BlockSpec((None, q, M2_INNER), lambda b, c: (b, cidx(c), 0))] + out_specs
        out_shape = [jax.ShapeDtypeStruct((b_, n, M2_INNER), BF16)] + out_shape
    return pl.pallas_call(
        kern,
        grid=(b_, nc),
        in_specs=[
            pl.BlockSpec((None, q, M2_CONV_DIM), lambda b, c: (b, cidx(c), 0)),
            pl.BlockSpec((None, q, 128), lambda b, c: (b, cidx(c), 0)),
            pl.BlockSpec((1, 128), lambda b, c: (0, 0)),
            pl.BlockSpec(emat.shape, lambda b, c: (0, 0)),
            pl.BlockSpec((None,) + sshape, lambda b, c: (b, 0, 0, 0)),
        ],
        out_specs=out_specs,
        out_shape=out_shape,
        scratch_shapes=[pltpu.VMEM(sshape, F32)],
        compiler_params=_cp(("arbitrary", "arbitrary")),
        name=("ssd_lat" if need_y else "ssd_ctx") + ("_bwd" if reverse else "_fwd"),
    )(xbc, dt, arow, emat, s0)


def _fin1_kernel(yf_ref, yb_ref, z_ref, xs_ref, x_ref, g1_ref, dx_ref, nw_ref, wout_ref, o_ref):
    y = yf_ref[...].astype(F32) + yb_ref[...].astype(F32) + dx_ref[...] * xs_ref[...].astype(F32)
    y = y * _silu(z_ref[...].astype(F32))
    ms = jnp.mean(y * y, axis=-1, keepdims=True)
    yn = (y * lax.rsqrt(ms + EPS)) * nw_ref[...]
    m = _dot(yn.astype(BF16), wout_ref[...])
    x = jnp.concatenate([x_ref[k] for k in range(x_ref.shape[0])], axis=0)
    o_ref[...] = x + g1_ref[...] * m


def _fin1(yf, yb, z, xbc, xcol, g1, dx, nw, wout, n_lat):
    b_, _, _, d = xcol.shape
    n = n_lat
    rows = n_lat // GRID_W
    tt = TOK_TILE
    const = lambda b, q: (0, 0)
    tok = lambda w: pl.BlockSpec((None, tt, w), lambda b, q: (b, q, 0))
    return pl.pallas_call(
        _fin1_kernel,
        grid=(b_, n // tt),
        in_specs=[
            tok(M2_INNER), tok(M2_INNER), tok(M2_INNER), tok(M2_INNER),
            pl.BlockSpec((None, tt // rows, rows, d), lambda b, q: (b, q, 0, 0)),
            pl.BlockSpec((None, 1, d), lambda b, q: (b, 0, 0)),
            pl.BlockSpec((1, M2_INNER), const),
            pl.BlockSpec((1, M2_INNER), const),
            pl.BlockSpec(wout.shape, const),
        ],
        out_specs=tok(d),
        out_shape=jax.ShapeDtypeStruct((b_, n, d), F32),
        compiler_params=_cp(("parallel", "parallel")),
        name="fin1",
    )(yf, yb, z, xbc, xcol, g1, dx, nw, wout)


def kernel(x, c, ctx, c_ctx, w_mod, b_mod, norm_mix, norm_ffn, norm_final, router_w, router_b, exp_w1, exp_w3, exp_w2, ab_w_in, ab_w_out, s5_lam_re, s5_lam_im, s5_log_dt, s5_b_re, s5_b_im, s5_c_re, s5_c_im, s5_d, s5_w_glu, rg_conv_w, rg_conv_b, rg_wa, rg_ba, rg_wx, rg_bx, rg_lam, m2_w_in, m2_conv_w, m2_conv_b, m2_dt_bias, m2_a_log, m2_d, m2_norm, m2_w_out):
    b_, n_lat, d = x.shape
    n_ctx = ctx.shape[1]
    assert d == D_MODEL and b_ <= 7
    assert n_lat % (GRID_W * HALO) == 0 and n_lat // GRID_W == M2_CHUNK
    assert n_lat % TOK_TILE == 0 and n_ctx % TOK_TILE == 0 and n_ctx % GRID_W == 0
    nt = n_lat + n_ctx

    xall = jnp.concatenate([x, ctx], axis=1)
    c8 = jnp.concatenate([c, c_ctx[None], jnp.zeros((7 - b_, d), F32)], axis=0)
    mods = _modulation(c8, w_mod, b_mod)
    nrow = b_ + 1

    def mod_rows(layer):
        m = mods[layer, :nrow].reshape(nrow, 6, 1, d)
        sh1, sc1, g1, sh2, sc2, g2 = (m[:, k] for k in range(6))
        return (norm_mix[layer] * (1.0 + sc1), sh1, g1, norm_ffn[layer] * (1.0 + sc2), sh2, g2)

    rwh, rwl = _split2(jnp.pad(router_w.astype(F32), ((0, 0), (0, 128 - N_EXPERTS))))
    rb = jnp.pad(router_b.astype(F32), (0, 128 - N_EXPERTS)).reshape(1, 128)
    one_row = jnp.ones((1, d), F32)

    sc1, sh1, g1, sc2, sh2, g2 = mod_rows(0)
    cw8 = jnp.pad(rg_conv_w[0].astype(F32), ((0, 8 - CONV_K), (0, 0)))
    u, xrow, xc, gr = _inproj0b(xall, sc1, sh1, ab_w_in[0].astype(BF16), cw8,
                                rg_conv_b[0].astype(F32).reshape(1, RG_WIDTH), n_lat)
    tmat, wout5, win5, aq = _s5_chunk_params(s5_lam_re[0], s5_lam_im[0], s5_log_dt[0], s5_b_re[0], s5_b_im[0],
                                             s5_c_re[0], s5_c_im[0], b_)
    wa, wx, ba, bx, sp = _rg_params_b(rg_wa[0], rg_ba[0], rg_wx[0], rg_bx[0], rg_lam[0])
    yx, sfr, sfi, sbr, sbi = _s5_local(xrow, tmat, wout5)
    hr, hi, gbr, gbi, hgf, hgb = _scan0b(sfr, sfi, sbr, sbi, xc, aq, wa, wx, ba, bx, sp, n_lat)
    yx = _s5_state_out(hr, hi, gbr, gbi, win5, yx)
    x1 = _merge0b(yx, hgf, hgb, u, gr, xall, g1, s5_d[0].astype(F32).reshape(1, S5_WIDTH),
                  s5_w_glu[0].astype(BF16), ab_w_out[0].astype(BF16), n_lat)
    ew1, ew3, ew2 = exp_w1.astype(BF16), exp_w3.astype(BF16), exp_w2.astype(BF16)
    x2 = _moe(x1, sc2, sh2, g2, rwh, rwl, rb, ew1, ew3, ew2, 0, one_row, n_lat, final=False)

    sc1, sh1, g1, sc2, sh2, g2 = mod_rows(1)
    w_in = m2_w_in[0]
    wz = w_in[:, :M2_INNER].astype(BF16)
    wxbc = w_in[:, M2_INNER:M2_INNER + M2_CONV_DIM].astype(BF16)
    wd = jnp.pad(w_in[:, M2_INNER + M2_CONV_DIM:], ((0, 0), (0, 128 - 2 * M2_HEADS))).astype(BF16)
    cw8 = jnp.pad(m2_conv_w[0].astype(F32), ((0, 8 - CONV_K), (0, 0)))
    cb = m2_conv_b[0].astype(F32).reshape(1, M2_CONV_DIM)
    db = jnp.pad(m2_dt_bias[0].astype(F32).reshape(2 * M2_HEADS), (0, 128 - 2 * M2_HEADS)).reshape(1, 128)
    grows = n_lat // GRID_W
    x2v = x2.reshape(b_, nt // GRID_W, GRID_W, d).transpose(0, 2, 1, 3)
    z_l, xbc_l, dt_l = _inproj1(x2, x2v, sc1, sh1, wz, wxbc, wd, cw8, cb, db, n_lat, latent=True)
    _, xbc_c, dt_c = _inproj1(x2, x2v, sc1, sh1, wz, wxbc, wd, cw8, cb, db, n_lat, latent=False)
    a_neg = -jnp.exp(m2_a_log[0].astype(F32))
    lanes = jnp.arange(128)
    heads = jnp.arange(M2_INNER) // M2_HEAD_DIM
    s0 = jnp.zeros((b_, M2_GROUPS, M2_STATE, M2_HPG * M2_HEAD_DIM), F32)
    ys = []
    for dirn in range(2):
        rev = dirn == 1
        arow = jnp.zeros((128,), F32).at[dirn * M2_HEADS:(dirn + 1) * M2_HEADS].set(a_neg[dirn]).reshape(1, 128)
        emat = (lanes[:, None] == heads[None, :] + dirn * M2_HEADS).astype(BF16)
        (st,) = _ssd(xbc_c, dt_c, arow, emat, s0, rev, need_y=False)
        y, _ = _ssd(xbc_l, dt_l, arow, emat, st, rev, need_y=True)
        ys.append(y)
    dx = jnp.repeat(m2_d[0].astype(F32), M2_HEAD_DIM).reshape(1, M2_INNER)
    x3 = _fin1(ys[0], ys[1], z_l, xbc_l, x2v, g1, dx,
               m2_norm[0].astype(F32).reshape(1, M2_INNER), m2_w_out[0].astype(BF16), n_lat)
    out = _moe(x3, sc2, sh2, g2, rwh, rwl, rb, ew1, ew3, ew2, 1,
               norm_final.astype(F32).reshape(1, d), n_lat, final=True)
    return out.reshape(b_, GRID_W, grows, d).transpose(0, 2, 1, 3).reshape(b_, n_lat, d)
```
